```python
import jax, jax.numpy as jnp
from jax import lax
import numpy as np

D_MODEL = 1024
BATCH = 8
SEQ = 4096
DEPTH = 2

CHUNK = 64
D_MIX = D_MODEL
A_WIDTH = D_MIX // 2
A_GROUPS = 8
A_GROUP_DIM = A_WIDTH // A_GROUPS
A_BLOCK = 128
B_WIDTH = D_MIX - A_WIDTH
B_HEADS = 4
B_HEAD_V = B_WIDTH // B_HEADS
B_HEAD_K = B_HEAD_V // 2
B_KEY = B_HEADS * B_HEAD_K
GATE_RANK = 16
GATE_NORM = 16.0
D_FF = 2816
EPS = 1e-6
SPLITS = (A_WIDTH, 2 * A_WIDTH, 2 * A_WIDTH + B_KEY, 2 * A_WIDTH + 2 * B_KEY,
          2 * A_WIDTH + 2 * B_KEY + B_WIDTH, 2 * A_WIDTH + 2 * B_KEY + 2 * B_WIDTH)
IN_COLS = 2 * A_WIDTH + 2 * B_KEY + 2 * B_WIDTH + GATE_RANK

kernel_name = "hybrid_gmlp_gla_macaron_encoder"


def rmsnorm(x, w):
    xf = x.astype(jnp.float32)
    y = xf * lax.rsqrt(jnp.mean(xf * xf, axis=-1, keepdims=True) + EPS)
    return (y * w.astype(jnp.float32)).astype(x.dtype)


def swiglu_ffn(h, w_in, w_out):
    a, gate = jnp.split(h @ w_in, 2, axis=-1)
    return (jax.nn.silu(gate) * a) @ w_out


def gmlp_spatial_gating(u, v, w_s, b_s, norm_v):
    bsz, seq, _ = u.shape
    nb = seq // A_BLOCK
    v = rmsnorm(v, norm_v).reshape(bsz, nb, A_BLOCK, A_GROUPS, A_GROUP_DIM)
    chunk_id = jnp.arange(A_BLOCK) // CHUNK
    mask = chunk_id[:, None] >= chunk_id[None, :]
    ws = jnp.where(mask[None], w_s, jnp.zeros_like(w_s))
    z = jnp.einsum('gts,bnsgc->bntgc', ws, v) + b_s.T[None, None, :, :, None].astype(v.dtype)
    return u * z.reshape(bsz, seq, A_WIDTH)


def gated_linear_attention(q, k, v, g, r, w_gk2, b_gk, norm_o):
    out_dtype = v.dtype
    bsz, seq, _ = q.shape
    nc = seq // CHUNK
    f32 = jnp.float32
    qc = q.astype(f32).reshape(bsz, nc, CHUNK, B_HEADS, B_HEAD_K) * (B_HEAD_K ** -0.5)
    kc = k.astype(f32).reshape(bsz, nc, CHUNK, B_HEADS, B_HEAD_K)
    vc = v.astype(f32).reshape(bsz, nc, CHUNK, B_HEADS, B_HEAD_V)
    log_a = jax.nn.log_sigmoid((r @ w_gk2 + b_gk).astype(f32)) / GATE_NORM
    log_a = log_a.reshape(bsz, nc, CHUNK, B_HEADS, B_HEAD_K)
    cum = jnp.cumsum(log_a, axis=2)
    tot = cum[:, :, -1]
    k_dec = kc * jnp.exp(tot[:, :, None] - cum)
    upd = jnp.einsum('bnthk,bnthv->bnhkv', k_dec, vc)
    decay = jnp.exp(tot)

    def step(state, inp):
        d, u = inp
        state = d[..., None] * state + u
        return state, state

    init = jnp.zeros((bsz, B_HEADS, B_HEAD_K, B_HEAD_V), f32)
    _, states = lax.scan(step, init, (jnp.moveaxis(decay, 1, 0), jnp.moveaxis(upd, 1, 0)))
    states = jnp.moveaxis(states, 0, 1)
    o = jnp.einsum('bnthk,bnhkv->bnthv', qc, states)
    o = o * lax.rsqrt(jnp.mean(o * o, axis=-1, keepdims=True) + EPS) * norm_o.astype(f32)
    o = o.reshape(bsz, seq, B_WIDTH).astype(out_dtype)
    return o * jax.nn.silu(g)


def setup_inputs(seed: int = 0) -> dict:
    key = jax.random.key(seed)
    ks = jax.random.split(key, 20)
    f32 = jnp.float32
    nrm = lambda k, shape, scale: jax.random.normal(k, shape, f32) * scale
    gain = lambda k, shape: 1.0 + 0.05 * jax.random.normal(k, shape, f32)
    L = DEPTH
    return {
        "x": nrm(ks[0], (BATCH, SEQ, D_MODEL), 1.0),
        "ff1_norm": gain(ks[1], (L, D_MODEL)),
        "ff1_w_in": nrm(ks[2], (L, D_MODEL, 2 * D_FF), D_MODEL ** -0.5),
        "ff1_w_out": nrm(ks[3], (L, D_FF, D_MODEL), D_FF ** -0.5),
        "mix_norm": gain(ks[4], (L, D_MODEL)),
        "w_in": nrm(ks[5], (L, D_MODEL, IN_COLS), D_MODEL ** -0.5),
        "gmlp_norm_v": gain(ks[6], (L, A_WIDTH)),
        "gmlp_w_s": nrm(ks[7], (L, A_GROUPS, A_BLOCK, A_BLOCK), A_BLOCK ** -0.5),
        "gmlp_b_s": 1.0 + 0.1 * jax.random.normal(ks[8], (L, A_GROUPS, A_BLOCK), f32),
        "gla_w_gk2": nrm(ks[9], (L, GATE_RANK, B_KEY), GATE_RANK ** -0.5),
        "gla_b_gk": nrm(ks[10], (L, B_KEY), 0.1),
        "gla_norm_o": gain(ks[11], (L, B_HEADS, B_HEAD_V)),
        "w_out": nrm(ks[12], (L, D_MIX, D_MODEL), D_MIX ** -0.5),
        "ff2_norm": gain(ks[13], (L, D_MODEL)),
        "ff2_w_in": nrm(ks[14], (L, D_MODEL, 2 * D_FF), D_MODEL ** -0.5),
        "ff2_w_out": nrm(ks[15], (L, D_FF, D_MODEL), D_FF ** -0.5),
        "final_norm": gain(ks[16], (D_MODEL,)),
    }


def reference(x, ff1_norm, ff1_w_in, ff1_w_out, mix_norm, w_in, gmlp_norm_v, gmlp_w_s,
              gmlp_b_s, gla_w_gk2, gla_b_gk, gla_norm_o, w_out, ff2_norm, ff2_w_in,
              ff2_w_out, final_norm):
    for l in range(DEPTH):
        x = x + 0.5 * swiglu_ffn(rmsnorm(x, ff1_norm[l]), ff1_w_in[l], ff1_w_out[l])
        h = rmsnorm(x, mix_norm[l])
        proj = h @ w_in[l]
        u, va, q, k, vb, g, r = jnp.split(proj, SPLITS, axis=-1)
        y_a = gmlp_spatial_gating(u, va, gmlp_w_s[l], gmlp_b_s[l], gmlp_norm_v[l])
        y_b = gated_linear_attention(q, k, vb, g, r, gla_w_gk2[l], gla_b_gk[l], gla_norm_o[l])
        x = x + jnp.concatenate([y_a, y_b], axis=-1) @ w_out[l]
        x = x + 0.5 * swiglu_ffn(rmsnorm(x, ff2_norm[l]), ff2_w_in[l], ff2_w_out[l])
    return rmsnorm(x, final_norm)
```

```python
import functools

import jax
import jax.numpy as jnp
from jax import lax
from jax.experimental import pallas as pl
from jax.experimental.pallas import tpu as pltpu

D_MODEL = 1024
D_FF = 2816
CHUNK = 64
A_WIDTH = 512
A_GROUPS = 8
A_GROUP_DIM = A_WIDTH // A_GROUPS
A_BLOCK = 128
B_WIDTH = 512
B_HEADS = 4
B_HEAD_V = 128
B_HEAD_K = 64
B_KEY = B_HEADS * B_HEAD_K
GATE_RANK = 16
GATE_NORM = 16.0
EPS = 1e-6

LANES = 128
R_PAD = LANES
IN_COLS_PAD = 2 * A_WIDTH + 2 * B_KEY + 2 * B_WIDTH + R_PAD
OFF_U = 0
OFF_VA = A_WIDTH
OFF_Q = 2 * A_WIDTH
OFF_K = OFF_Q + B_KEY
OFF_VB = OFF_K + B_KEY
OFF_G = OFF_VB + B_WIDTH
OFF_R = OFF_G + B_WIDTH
OFF_LA = OFF_U

FF_CHUNK = 256
N_FF_CHUNKS = D_FF // FF_CHUNK
FFN_ROWS = 1024
MIX_ROWS = 512
VMEM_LIMIT = 56 * 1024 * 1024

BF16 = jnp.bfloat16
F32 = jnp.float32


def _rms_scale(x):
    return lax.rsqrt(jnp.mean(x * x, axis=-1, keepdims=True) + EPS)


def _sigmoid(x):
    return 1.0 / (1.0 + jnp.exp(-x))


def _ffn_kernel(x_ref, nw_ref, win_ref, wout_ref, fw_ref, o_ref, xn_ref, acc_ref, *, final_norm):
    x = x_ref[...]
    xn_ref[...] = (x * _rms_scale(x) * nw_ref[...]).astype(BF16)
    acc_ref[...] = jnp.zeros_like(acc_ref)

    def body(c, carry):
        h = jnp.dot(xn_ref[...], win_ref[c], preferred_element_type=F32)
        a = h[:, :FF_CHUNK]
        g = h[:, FF_CHUNK:]
        hh = (g * _sigmoid(g) * a).astype(BF16)
        acc_ref[...] += jnp.dot(hh, wout_ref[c], preferred_element_type=F32)
        return carry

    lax.fori_loop(0, N_FF_CHUNKS, body, 0)
    y = x_ref[...] + 0.5 * acc_ref[...]
    if final_norm:
        y = y * _rms_scale(y) * fw_ref[...]
    o_ref[...] = y


def _resident(shape):
    zeros = (0,) * len(shape)
    return pl.BlockSpec(shape, lambda *_: zeros, pipeline_mode=pl.Buffered(1))


def ffn_call(x2d, norm_w, w_in_c, w_out_c, final_w, *, final_norm):
    n = x2d.shape[0]
    return pl.pallas_call(
        functools.partial(_ffn_kernel, final_norm=final_norm),
        out_shape=jax.ShapeDtypeStruct(x2d.shape, F32),
        grid=(n // FFN_ROWS,),
        in_specs=[
            pl.BlockSpec((FFN_ROWS, D_MODEL), lambda i: (i, 0)),
            _resident((1, D_MODEL)),
            _resident(w_in_c.shape),
            _resident(w_out_c.shape),
            _resident((1, D_MODEL)),
        ],
        out_specs=pl.BlockSpec((FFN_ROWS, D_MODEL), lambda i: (i, 0)),
        scratch_shapes=[
            pltpu.VMEM((FFN_ROWS, D_MODEL), BF16),
            pltpu.VMEM((FFN_ROWS, D_MODEL), F32),
        ],
        compiler_params=pltpu.CompilerParams(
            dimension_semantics=("arbitrary",), vmem_limit_bytes=VMEM_LIMIT),
        name="ffn",
    )(x2d, norm_w, w_in_c, w_out_c, final_w)


def _mixer_kernel(x_ref, nw_ref, win_ref, nv_ref, ws_ref, bs_ref, wgk_ref, bgk_ref, no_ref,
                  wout_ref, o_ref, p_ref, y_ref, st_ref):
    rows = x_ref.shape[0]

    @pl.when(pl.program_id(1) == 0)
    def _():
        st_ref[...] = jnp.zeros_like(st_ref)

    x = x_ref[...]
    h = (x * _rms_scale(x) * nw_ref[...]).astype(BF16)
    p_ref[...] = jnp.dot(h, win_ref[...], preferred_element_type=F32)

    va = p_ref[:, OFF_VA:OFF_VA + A_WIDTH]
    vn = (va * _rms_scale(va) * nv_ref[...]).astype(BF16)
    t_chunk = lax.broadcasted_iota(jnp.int32, (A_BLOCK, A_BLOCK), 0) // CHUNK
    s_chunk = lax.broadcasted_iota(jnp.int32, (A_BLOCK, A_BLOCK), 1) // CHUNK
    causal = t_chunk >= s_chunk
    ws_cat = jnp.concatenate(
        [jnp.where(causal, ws_ref[g], 0.0).astype(BF16) for g in range(A_GROUPS)], axis=1)
    lane_group = lax.broadcasted_iota(jnp.int32, (A_BLOCK, A_WIDTH), 1) // A_GROUP_DIM
    for blk in range(rows // A_BLOCK):
        r0 = blk * A_BLOCK
        v_blk = vn[r0:r0 + A_BLOCK, :]
        v_stack = jnp.concatenate(
            [jnp.where(lane_group == g, v_blk, jnp.zeros_like(v_blk)) for g in range(A_GROUPS)],
            axis=0)
        z = jnp.dot(ws_cat, v_stack, preferred_element_type=F32) + bs_ref[...]
        u = p_ref[r0:r0 + A_BLOCK, OFF_U:OFF_U + A_WIDTH]
        y_ref[r0:r0 + A_BLOCK, 0:A_WIDTH] = (u * z).astype(BF16)

    r = p_ref[:, OFF_R:OFF_R + R_PAD].astype(BF16)
    gate_pre = jnp.dot(r, wgk_ref[...], preferred_element_type=F32) + bgk_ref[...]
    log_a = (jnp.minimum(gate_pre, 0.0) - jnp.log1p(jnp.exp(-jnp.abs(gate_pre)))) / GATE_NORM
    p_ref[:, OFF_LA:OFF_LA + B_KEY] = log_a

    tri = (lax.broadcasted_iota(jnp.int32, (CHUNK, CHUNK), 0)
           >= lax.broadcasted_iota(jnp.int32, (CHUNK, CHUNK), 1)).astype(F32)
    pair_rows = 2 * B_HEAD_V
    pair_lanes = 2 * B_HEAD_K
    same_head = (lax.broadcasted_iota(jnp.int32, (pair_rows, pair_lanes), 0) // B_HEAD_V
                 == lax.broadcasted_iota(jnp.int32, (pair_rows, pair_lanes), 1) // B_HEAD_K)

    def chunk_body(c, carry):
        t0 = pl.multiple_of(c * CHUNK, CHUNK)
        la = p_ref[pl.ds(t0, CHUNK), OFF_LA:OFF_LA + B_KEY]
        cum = jnp.dot(tri, la, preferred_element_type=F32, precision=lax.Precision.HIGHEST)
        tot = cum[CHUNK - 1:CHUNK, :]
        k_dec = (p_ref[pl.ds(t0, CHUNK), OFF_K:OFF_K + B_KEY] * jnp.exp(tot - cum)).astype(BF16)
        decay = jnp.exp(tot)
        q = (p_ref[pl.ds(t0, CHUNK), OFF_Q:OFF_Q + B_KEY] * (B_HEAD_K ** -0.5)).astype(BF16)
        v = p_ref[pl.ds(t0, CHUNK), OFF_VB:OFF_VB + B_WIDTH].astype(BF16)
        for pr in range(B_HEADS // 2):
            kl = slice(pr * pair_lanes, (pr + 1) * pair_lanes)
            vl = slice(pr * pair_rows, (pr + 1) * pair_rows)
            upd = lax.dot_general(v[:, vl], k_dec[:, kl], (((0,), (0,)), ((), ())),
                                  preferred_element_type=F32)
            st = st_ref[pr] * decay[:, kl] + jnp.where(same_head, upd, 0.0)
            st_ref[pr] = st
            o = lax.dot_general(q[:, kl], st.astype(BF16), (((1,), (1,)), ((), ())),
                                preferred_element_type=F32)
            p_ref[pl.ds(t0, CHUNK), OFF_VB + pr * pair_rows:OFF_VB + (pr + 1) * pair_rows] = o
        return carry

    lax.fori_loop(0, rows // CHUNK, chunk_body, 0)

    for hd in range(B_HEADS):
        cols = slice(OFF_VB + hd * B_HEAD_V, OFF_VB + (hd + 1) * B_HEAD_V)
        o = p_ref[:, cols]
        o = o * _rms_scale(o) * no_ref[:, hd * B_HEAD_V:(hd + 1) * B_HEAD_V]
        g = p_ref[:, OFF_G + hd * B_HEAD_V:OFF_G + (hd + 1) * B_HEAD_V]
        y_ref[:, A_WIDTH + hd * B_HEAD_V:A_WIDTH + (hd + 1) * B_HEAD_V] = (
            o * (g * _sigmoid(g))).astype(BF16)

    o_ref[...] = x_ref[...] + jnp.dot(y_ref[...], wout_ref[...], preferred_element_type=F32)


def mixer_call(x, norm_w, w_in_p, norm_v, w_s, b_s_full, w_gk_p, b_gk, norm_o, w_out):
    bsz, seq, _ = x.shape
    n_pairs = B_HEADS // 2
    return pl.pallas_call(
        _mixer_kernel,
        out_shape=jax.ShapeDtypeStruct(x.shape, F32),
        grid=(bsz, seq // MIX_ROWS),
        in_specs=[
            pl.BlockSpec((None, MIX_ROWS, D_MODEL), lambda b, s: (b, s, 0)),
            _resident((1, D_MODEL)),
            _resident(w_in_p.shape),
            _resident((1, A_WIDTH)),
            _resident(w_s.shape),
            _resident(b_s_full.shape),
            _resident(w_gk_p.shape),
            _resident((1, B_KEY)),
            _resident((1, B_WIDTH)),
            _resident(w_out.shape),
        ],
        out_specs=pl.BlockSpec((None, MIX_ROWS, D_MODEL), lambda b, s: (b, s, 0)),
        scratch_shapes=[
            pltpu.VMEM((MIX_ROWS, IN_COLS_PAD), F32),
            pltpu.VMEM((MIX_ROWS, D_MODEL), BF16),
            pltpu.VMEM((n_pairs, 2 * B_HEAD_V, 2 * B_HEAD_K), F32),
        ],
        compiler_params=pltpu.CompilerParams(
            dimension_semantics=("arbitrary", "arbitrary"), vmem_limit_bytes=VMEM_LIMIT),
        name="mixer",
    )(x, norm_w, w_in_p, norm_v, w_s, b_s_full, w_gk_p, b_gk, norm_o, w_out)


def _ffn_weights(w_in, w_out):
    a = w_in[:, :D_FF].reshape(D_MODEL, N_FF_CHUNKS, FF_CHUNK)
    g = w_in[:, D_FF:].reshape(D_MODEL, N_FF_CHUNKS, FF_CHUNK)
    w_in_c = jnp.concatenate([a, g], axis=-1).transpose(1, 0, 2).astype(BF16)
    w_out_c = w_out.reshape(N_FF_CHUNKS, FF_CHUNK, D_MODEL).astype(BF16)
    return w_in_c, w_out_c


def kernel(x, ff1_norm, ff1_w_in, ff1_w_out, mix_norm, w_in, gmlp_norm_v, gmlp_w_s, gmlp_b_s,
           gla_w_gk2, gla_b_gk, gla_norm_o, w_out, ff2_norm, ff2_w_in, ff2_w_out, final_norm):
    bsz, seq, d = x.shape
    depth = ff1_norm.shape[0]
    row = lambda v: v.reshape(1, -1)
    fw = row(final_norm)
    for l in range(depth):
        w1_in, w1_out = _ffn_weights(ff1_w_in[l], ff1_w_out[l])
        w2_in, w2_out = _ffn_weights(ff2_w_in[l], ff2_w_out[l])
        w_in_p = jnp.pad(w_in[l], ((0, 0), (0, R_PAD - GATE_RANK))).astype(BF16)
        w_gk_p = jnp.pad(gla_w_gk2[l], ((0, R_PAD - GATE_RANK), (0, 0))).astype(BF16)
        b_s_full = jnp.repeat(gmlp_b_s[l].T, A_GROUP_DIM, axis=1)

        x = ffn_call(x.reshape(bsz * seq, d), row(ff1_norm[l]), w1_in, w1_out, fw,
                     final_norm=False).reshape(bsz, seq, d)
        x = mixer_call(x, row(mix_norm[l]), w_in_p, row(gmlp_norm_v[l]), gmlp_w_s[l], b_s_full,
                       w_gk_p, row(gla_b_gk[l]), row(gla_norm_o[l]), w_out[l].astype(BF16))
        x = ffn_call(x.reshape(bsz * seq, d), row(ff2_norm[l]), w2_in, w2_out, fw,
                     final_norm=(l == depth - 1)).reshape(bsz, seq, d)
    return x
```

```python
import functools

import jax
import jax.numpy as jnp
from jax import lax
from jax.experimental import pallas as pl
from jax.experimental.pallas import tpu as pltpu

D_MODEL = 1024
D_FF = 2816
CHUNK = 64
A_WIDTH = 512
A_GROUPS = 8
A_GROUP_DIM = A_WIDTH // A_GROUPS
A_BLOCK = 128
B_WIDTH = 512
B_HEADS = 4
B_HEAD_V = 128
B_HEAD_K = 64
B_KEY = B_HEADS * B_HEAD_K
GATE_RANK = 16
GATE_NORM = 16.0
EPS = 1e-6

LANES = 128
R_PAD = LANES
IN_COLS_PAD = 2 * A_WIDTH + 2 * B_KEY + 2 * B_WIDTH + R_PAD
OFF_U = 0
OFF_VA = A_WIDTH
OFF_Q = 2 * A_WIDTH
OFF_K = OFF_Q + B_KEY
OFF_VB = OFF_K + B_KEY
OFF_G = OFF_VB + B_WIDTH
OFF_R = OFF_G + B_WIDTH

FF_CHUNK = 256
N_FF_CHUNKS = D_FF // FF_CHUNK
FFN_ROWS = 1024
MIX_ROWS = 512
VMEM_LIMIT = 56 * 1024 * 1024

PAIR_ROWS = 2 * B_HEAD_V
PAIR_LANES = 2 * B_HEAD_K
N_PAIRS = B_HEADS // 2

BF16 = jnp.bfloat16
F32 = jnp.float32


def _rms_scale(x):
    return lax.rsqrt(jnp.mean(x * x, axis=-1, keepdims=True) + EPS)


def _sigmoid(x):
    return 1.0 / (1.0 + jnp.exp(-x))


def _split3(x):
    hi = x.astype(BF16)
    r1 = x - hi.astype(F32)
    mid = r1.astype(BF16)
    lo = (r1 - mid.astype(F32)).astype(BF16)
    return hi, mid, lo


def _ffn_kernel(x_ref, nw_ref, win_ref, wout_ref, fw_ref, o_ref, xn_ref, acc_ref, *, final_norm):
    x = x_ref[...]
    xn_ref[...] = (x * _rms_scale(x) * nw_ref[...]).astype(BF16)
    acc_ref[...] = jnp.zeros_like(acc_ref)

    def body(c, carry):
        h = jnp.dot(xn_ref[...], win_ref[c], preferred_element_type=F32)
        a = h[:, :FF_CHUNK]
        g = h[:, FF_CHUNK:]
        hh = (g * _sigmoid(g) * a).astype(BF16)
        acc_ref[...] += jnp.dot(hh, wout_ref[c], preferred_element_type=F32)
        return carry

    lax.fori_loop(0, N_FF_CHUNKS, body, 0)
    y = x_ref[...] + 0.5 * acc_ref[...]
    if final_norm:
        y = y * _rms_scale(y) * fw_ref[...]
    o_ref[...] = y


def _resident(shape):
    zeros = (0,) * len(shape)
    return pl.BlockSpec(shape, lambda *_: zeros, pipeline_mode=pl.Buffered(1))


def ffn_call(x2d, norm_w, w_in_c, w_out_c, final_w, *, final_norm):
    n = x2d.shape[0]
    return pl.pallas_call(
        functools.partial(_ffn_kernel, final_norm=final_norm),
        out_shape=jax.ShapeDtypeStruct(x2d.shape, F32),
        grid=(n // FFN_ROWS,),
        in_specs=[
            pl.BlockSpec((FFN_ROWS, D_MODEL), lambda i: (i, 0)),
            _resident((1, D_MODEL)),
            _resident(w_in_c.shape),
            _resident(w_out_c.shape),
            _resident((1, D_MODEL)),
        ],
        out_specs=pl.BlockSpec((FFN_ROWS, D_MODEL), lambda i: (i, 0)),
        scratch_shapes=[
            pltpu.VMEM((FFN_ROWS, D_MODEL), BF16),
            pltpu.VMEM((FFN_ROWS, D_MODEL), F32),
        ],
        compiler_params=pltpu.CompilerParams(
            dimension_semantics=("arbitrary",), vmem_limit_bytes=VMEM_LIMIT),
        name="ffn",
    )(x2d, norm_w, w_in_c, w_out_c, final_w)


def _mixer_kernel(x_ref, nw_ref, win_ref, nv_ref, ws_ref, bs_ref, wgk_ref, bgk_ref, no_ref,
                  wout_ref, o_ref, p_ref, y_ref, upd_ref, st_ref):
    rows = x_ref.shape[0]
    n_chunks = rows // CHUNK

    @pl.when(pl.program_id(1) == 0)
    def _():
        st_ref[...] = jnp.zeros_like(st_ref)

    x = x_ref[...]
    h = (x * _rms_scale(x) * nw_ref[...]).astype(BF16)
    p_ref[...] = jnp.dot(h, win_ref[...], preferred_element_type=F32)

    va = p_ref[:, OFF_VA:OFF_VA + A_WIDTH]
    vn = (va * _rms_scale(va) * nv_ref[...]).astype(BF16)
    t_chunk = lax.broadcasted_iota(jnp.int32, (A_BLOCK, A_BLOCK), 0) // CHUNK
    s_chunk = lax.broadcasted_iota(jnp.int32, (A_BLOCK, A_BLOCK), 1) // CHUNK
    causal = t_chunk >= s_chunk
    half_groups = A_GROUPS // 2
    half_width = A_WIDTH // 2
    ws_half = [
        jnp.concatenate([jnp.where(causal, ws_ref[hf * half_groups + g], 0.0).astype(BF16)
                         for g in range(half_groups)], axis=1)
        for hf in range(2)]
    lane_group = lax.broadcasted_iota(jnp.int32, (A_BLOCK, half_width), 1) // A_GROUP_DIM
    for blk in range(rows // A_BLOCK):
        r0 = blk * A_BLOCK
        for hf in range(2):
            c0 = hf * half_width
            v_blk = vn[r0:r0 + A_BLOCK, c0:c0 + half_width]
            v_stack = jnp.concatenate(
                [jnp.where(lane_group == g, v_blk, jnp.zeros_like(v_blk))
                 for g in range(half_groups)], axis=0)
            z = (jnp.dot(ws_half[hf], v_stack, preferred_element_type=F32)
                 + bs_ref[:, c0:c0 + half_width])
            u = p_ref[r0:r0 + A_BLOCK, OFF_U + c0:OFF_U + c0 + half_width]
            y_ref[r0:r0 + A_BLOCK, c0:c0 + half_width] = (u * z).astype(BF16)

    r = p_ref[:, OFF_R:OFF_R + R_PAD].astype(BF16)
    gate_pre = jnp.dot(r, wgk_ref[...], preferred_element_type=F32) + bgk_ref[...]
    log_a = (jnp.minimum(gate_pre, 0.0) - jnp.log1p(jnp.exp(-jnp.abs(gate_pre)))) / GATE_NORM

    tri = (lax.broadcasted_iota(jnp.int32, (CHUNK, CHUNK), 0)
           >= lax.broadcasted_iota(jnp.int32, (CHUNK, CHUNK), 1)).astype(BF16)
    same_head = (lax.broadcasted_iota(jnp.int32, (PAIR_ROWS, PAIR_LANES), 0) // B_HEAD_V
                 == lax.broadcasted_iota(jnp.int32, (PAIR_ROWS, PAIR_LANES), 1) // B_HEAD_K)

    decays = []
    for c in range(n_chunks):
        t0 = c * CHUNK
        la = log_a[t0:t0 + CHUNK, :]
        cum3 = jnp.dot(tri, jnp.concatenate(_split3(la), axis=1), preferred_element_type=F32)
        cum = cum3[:, :B_KEY] + cum3[:, B_KEY:2 * B_KEY] + cum3[:, 2 * B_KEY:]
        tot = cum[CHUNK - 1:CHUNK, :]
        decays.append(jnp.exp(tot))
        k_dec = (p_ref[t0:t0 + CHUNK, OFF_K:OFF_K + B_KEY] * jnp.exp(tot - cum)).astype(BF16)
        v = p_ref[t0:t0 + CHUNK, OFF_VB:OFF_VB + B_WIDTH].astype(BF16)
        for pr in range(N_PAIRS):
            kl = slice(pr * PAIR_LANES, (pr + 1) * PAIR_LANES)
            vl = slice(pr * PAIR_ROWS, (pr + 1) * PAIR_ROWS)
            upd = lax.dot_general(v[:, vl], k_dec[:, kl], (((0,), (0,)), ((), ())),
                                  preferred_element_type=F32)
            upd_ref[c, pr] = jnp.where(same_head, upd, 0.0)

    for pr in range(N_PAIRS):
        kl = slice(pr * PAIR_LANES, (pr + 1) * PAIR_LANES)
        st = st_ref[pr]
        for c in range(n_chunks):
            t0 = c * CHUNK
            st = st * decays[c][:, kl] + upd_ref[c, pr]
            q = (p_ref[t0:t0 + CHUNK, OFF_Q + pr * PAIR_LANES:OFF_Q + (pr + 1) * PAIR_LANES]
                 * (B_HEAD_K ** -0.5)).astype(BF16)
            o = lax.dot_general(q, st.astype(BF16), (((1,), (1,)), ((), ())),
                                preferred_element_type=F32)
            p_ref[t0:t0 + CHUNK, OFF_VB + pr * PAIR_ROWS:OFF_VB + (pr + 1) * PAIR_ROWS] = o
        st_ref[pr] = st

    for hd in range(B_HEADS):
        cols = slice(OFF_VB + hd * B_HEAD_V, OFF_VB + (hd + 1) * B_HEAD_V)
        o = p_ref[:, cols]
        o = o * _rms_scale(o) * no_ref[:, hd * B_HEAD_V:(hd + 1) * B_HEAD_V]
        g = p_ref[:, OFF_G + hd * B_HEAD_V:OFF_G + (hd + 1) * B_HEAD_V]
        y_ref[:, A_WIDTH + hd * B_HEAD_V:A_WIDTH + (hd + 1) * B_HEAD_V] = (
            o * (g * _sigmoid(g))).astype(BF16)

    o_ref[...] = x_ref[...] + jnp.dot(y_ref[...], wout_ref[...], preferred_element_type=F32)


def mixer_call(x, norm_w, w_in_p, norm_v, w_s, b_s_full, w_gk_p, b_gk, norm_o, w_out):
    bsz, seq, _ = x.shape
    return pl.pallas_call(
        _mixer_kernel,
        out_shape=jax.ShapeDtypeStruct(x.shape, F32),
        grid=(bsz, seq // MIX_ROWS),
        in_specs=[
            pl.BlockSpec((None, MIX_ROWS, D_MODEL), lambda b, s: (b, s, 0)),
            _resident((1, D_MODEL)),
            _resident(w_in_p.shape),
            _resident((1, A_WIDTH)),
            _resident(w_s.shape),
            _resident(b_s_full.shape),
            _resident(w_gk_p.shape),
            _resident((1, B_KEY)),
            _resident((1, B_WIDTH)),
            _resident(w_out.shape),
        ],
        out_specs=pl.BlockSpec((None, MIX_ROWS, D_MODEL), lambda b, s: (b, s, 0)),
        scratch_shapes=[
            pltpu.VMEM((MIX_ROWS, IN_COLS_PAD), F32),
            pltpu.VMEM((MIX_ROWS, D_MODEL), BF16),
            pltpu.VMEM((MIX_ROWS // CHUNK, N_PAIRS, PAIR_ROWS, PAIR_LANES), F32),
            pltpu.VMEM((N_PAIRS, PAIR_ROWS, PAIR_LANES), F32),
        ],
        compiler_params=pltpu.CompilerParams(
            dimension_semantics=("arbitrary", "arbitrary"), vmem_limit_bytes=VMEM_LIMIT),
        name="mixer",
    )(x, norm_w, w_in_p, norm_v, w_s, b_s_full, w_gk_p, b_gk, norm_o, w_out)


def _ffn_weights(w_in, w_out):
    a = w_in[:, :D_FF].reshape(D_MODEL, N_FF_CHUNKS, FF_CHUNK)
    g = w_in[:, D_FF:].reshape(D_MODEL, N_FF_CHUNKS, FF_CHUNK)
    w_in_c = jnp.concatenate([a, g], axis=-1).transpose(1, 0, 2).astype(BF16)
    w_out_c = w_out.reshape(N_FF_CHUNKS, FF_CHUNK, D_MODEL).astype(BF16)
    return w_in_c, w_out_c


def kernel(x, ff1_norm, ff1_w_in, ff1_w_out, mix_norm, w_in, gmlp_norm_v, gmlp_w_s, gmlp_b_s,
           gla_w_gk2, gla_b_gk, gla_norm_o, w_out, ff2_norm, ff2_w_in, ff2_w_out, final_norm):
    bsz, seq, d = x.shape
    depth = ff1_norm.shape[0]
    row = lambda v: v.reshape(1, -1)
    fw = row(final_norm)
    for l in range(depth):
        w1_in, w1_out = _ffn_weights(ff1_w_in[l], ff1_w_out[l])
        w2_in, w2_out = _ffn_weights(ff2_w_in[l], ff2_w_out[l])
        w_in_p = jnp.pad(w_in[l], ((0, 0), (0, R_PAD - GATE_RANK))).astype(BF16)
        w_gk_p = jnp.pad(gla_w_gk2[l], ((0, R_PAD - GATE_RANK), (0, 0))).astype(BF16)
        b_s_full = jnp.repeat(gmlp_b_s[l].T, A_GROUP_DIM, axis=1)

        x = ffn_call(x.reshape(bsz * seq, d), row(ff1_norm[l]), w1_in, w1_out, fw,
                     final_norm=False).reshape(bsz, seq, d)
        x = mixer_call(x, row(mix_norm[l]), w_in_p, row(gmlp_norm_v[l]), gmlp_w_s[l], b_s_full,
                       w_gk_p, row(gla_b_gk[l]), row(gla_norm_o[l]), w_out[l].astype(BF16))
        x = ffn_call(x.reshape(bsz * seq, d), row(ff2_norm[l]), w2_in, w2_out, fw,
                     final_norm=(l == depth - 1)).reshape(bsz, seq, d)
    return x
```

```python
import functools

import jax
import jax.numpy as jnp
from jax import lax
from jax.experimental import pallas as pl
from jax.experimental.pallas import tpu as pltpu

D_MODEL = 1024
D_FF = 2816
CHUNK = 64
A_WIDTH = 512
A_GROUPS = 8
A_GROUP_DIM = A_WIDTH // A_GROUPS
A_BLOCK = 128
B_WIDTH = 512
B_HEADS = 4
B_HEAD_V = 128
B_HEAD_K = 64
B_KEY = B_HEADS * B_HEAD_K
GATE_RANK = 16
GATE_NORM = 16.0
EPS = 1e-6

LANES = 128
R_PAD = LANES
IN_COLS_PAD = 2 * A_WIDTH + 2 * B_KEY + 2 * B_WIDTH + R_PAD
OFF_U = 0
OFF_VA = A_WIDTH
OFF_Q = 2 * A_WIDTH
OFF_K = OFF_Q + B_KEY
OFF_VB = OFF_K + B_KEY
OFF_G = OFF_VB + B_WIDTH
OFF_R = OFF_G + B_WIDTH

FF_CHUNK = 256
N_FF_CHUNKS = D_FF // FF_CHUNK
FFN_ROWS = 1024
MIX_ROWS = 512
VMEM_LIMIT = 56 * 1024 * 1024

PAIR_ROWS = 2 * B_HEAD_V
PAIR_LANES = 2 * B_HEAD_K
N_PAIRS = B_HEADS // 2

BF16 = jnp.bfloat16
F32 = jnp.float32


def _rms_scale(x):
    return lax.rsqrt(jnp.mean(x * x, axis=-1, keepdims=True) + EPS)


def _sigmoid(x):
    return 1.0 / (1.0 + jnp.exp(-x))


def _split3(x):
    hi = x.astype(BF16)
    r1 = x - hi.astype(F32)
    mid = r1.astype(BF16)
    lo = (r1 - mid.astype(F32)).astype(BF16)
    return hi, mid, lo


def _ffn_kernel(x_ref, nw_ref, win_ref, wout_ref, fw_ref, o_ref, xn_ref, acc_ref, *, final_norm):
    x = x_ref[...]
    xn_ref[...] = (x * _rms_scale(x) * nw_ref[...]).astype(BF16)
    for c in range(N_FF_CHUNKS):
        lo = c * FF_CHUNK
        a = jnp.dot(xn_ref[...], win_ref[:, lo:lo + FF_CHUNK], preferred_element_type=F32)
        g = jnp.dot(xn_ref[...], win_ref[:, D_FF + lo:D_FF + lo + FF_CHUNK],
                    preferred_element_type=F32)
        hh = (g * _sigmoid(g) * a).astype(BF16)
        part = jnp.dot(hh, wout_ref[lo:lo + FF_CHUNK, :], preferred_element_type=F32)
        if c == 0:
            acc_ref[...] = part
        elif c < N_FF_CHUNKS - 1:
            acc_ref[...] += part
        else:
            y = x_ref[...] + 0.5 * (acc_ref[...] + part)
            if final_norm:
                y = y * _rms_scale(y) * fw_ref[...]
            o_ref[...] = y


def _resident(shape):
    zeros = (0,) * len(shape)
    return pl.BlockSpec(shape, lambda *_: zeros, pipeline_mode=pl.Buffered(1))


def ffn_call(x2d, norm_w, w_in_c, w_out_c, final_w, *, final_norm):
    n = x2d.shape[0]
    return pl.pallas_call(
        functools.partial(_ffn_kernel, final_norm=final_norm),
        out_shape=jax.ShapeDtypeStruct(x2d.shape, F32),
        grid=(n // FFN_ROWS,),
        in_specs=[
            pl.BlockSpec((FFN_ROWS, D_MODEL), lambda i: (i, 0)),
            _resident((1, D_MODEL)),
            _resident(w_in_c.shape),
            _resident(w_out_c.shape),
            _resident((1, D_MODEL)),
        ],
        out_specs=pl.BlockSpec((FFN_ROWS, D_MODEL), lambda i: (i, 0)),
        scratch_shapes=[
            pltpu.VMEM((FFN_ROWS, D_MODEL), BF16),
            pltpu.VMEM((FFN_ROWS, D_MODEL), F32),
        ],
        compiler_params=pltpu.CompilerParams(
            dimension_semantics=("arbitrary",), vmem_limit_bytes=VMEM_LIMIT),
        name="ffn",
    )(x2d, norm_w, w_in_c, w_out_c, final_w)


def _mixer_kernel(x_ref, nw_ref, win_ref, nv_ref, ws_ref, bs_ref, wgk_ref, bgk_ref, no_ref,
                  wout_ref, o_ref, p_ref, y_ref, upd_ref, st_ref):
    rows = x_ref.shape[0]
    n_chunks = rows // CHUNK

    @pl.when(pl.program_id(1) == 0)
    def _():
        st_ref[...] = jnp.zeros_like(st_ref)

    x = x_ref[...]
    h = (x * _rms_scale(x) * nw_ref[...]).astype(BF16)
    p_ref[...] = jnp.dot(h, win_ref[...], preferred_element_type=F32)

    va = p_ref[:, OFF_VA:OFF_VA + A_WIDTH]
    vn = (va * _rms_scale(va) * nv_ref[...]).astype(BF16)
    t_chunk = lax.broadcasted_iota(jnp.int32, (A_BLOCK, A_BLOCK), 0) // CHUNK
    s_chunk = lax.broadcasted_iota(jnp.int32, (A_BLOCK, A_BLOCK), 1) // CHUNK
    causal = t_chunk >= s_chunk
    half_groups = A_GROUPS // 2
    half_width = A_WIDTH // 2
    ws_half = [
        jnp.concatenate([jnp.where(causal, ws_ref[hf * half_groups + g], 0.0).astype(BF16)
                         for g in range(half_groups)], axis=1)
        for hf in range(2)]
    lane_group = lax.broadcasted_iota(jnp.int32, (A_BLOCK, half_width), 1) // A_GROUP_DIM
    for blk in range(rows // A_BLOCK):
        r0 = blk * A_BLOCK
        for hf in range(2):
            c0 = hf * half_width
            v_blk = vn[r0:r0 + A_BLOCK, c0:c0 + half_width]
            v_stack = jnp.concatenate(
                [jnp.where(lane_group == g, v_blk, jnp.zeros_like(v_blk))
                 for g in range(half_groups)], axis=0)
            z = (jnp.dot(ws_half[hf], v_stack, preferred_element_type=F32)
                 + bs_ref[:, c0:c0 + half_width])
            u = p_ref[r0:r0 + A_BLOCK, OFF_U + c0:OFF_U + c0 + half_width]
            y_ref[r0:r0 + A_BLOCK, c0:c0 + half_width] = (u * z).astype(BF16)

    r = p_ref[:, OFF_R:OFF_R + R_PAD].astype(BF16)
    gate_pre = jnp.dot(r, wgk_ref[...], preferred_element_type=F32) + bgk_ref[...]
    log_a = (jnp.minimum(gate_pre, 0.0) - jnp.log1p(jnp.exp(-jnp.abs(gate_pre)))) / GATE_NORM

    tri = (lax.broadcasted_iota(jnp.int32, (CHUNK, CHUNK), 0)
           >= lax.broadcasted_iota(jnp.int32, (CHUNK, CHUNK), 1)).astype(BF16)
    same_head = (lax.broadcasted_iota(jnp.int32, (PAIR_ROWS, PAIR_LANES), 0) // B_HEAD_V
                 == lax.broadcasted_iota(jnp.int32, (PAIR_ROWS, PAIR_LANES), 1) // B_HEAD_K)

    decays = []
    for c in range(n_chunks):
        t0 = c * CHUNK
        la = log_a[t0:t0 + CHUNK, :]
        cum3 = jnp.dot(tri, jnp.concatenate(_split3(la), axis=1), preferred_element_type=F32)
        cum = cum3[:, :B_KEY] + cum3[:, B_KEY:2 * B_KEY] + cum3[:, 2 * B_KEY:]
        tot = cum[CHUNK - 1:CHUNK, :]
        decays.append(jnp.exp(tot))
        k_dec = (p_ref[t0:t0 + CHUNK, OFF_K:OFF_K + B_KEY] * jnp.exp(tot - cum)).astype(BF16)
        v = p_ref[t0:t0 + CHUNK, OFF_VB:OFF_VB + B_WIDTH].astype(BF16)
        for pr in range(N_PAIRS):
            kl = slice(pr * PAIR_LANES, (pr + 1) * PAIR_LANES)
            vl = slice(pr * PAIR_ROWS, (pr + 1) * PAIR_ROWS)
            upd = lax.dot_general(v[:, vl], k_dec[:, kl], (((0,), (0,)), ((), ())),
                                  preferred_element_type=F32)
            upd_ref[c, pr] = jnp.where(same_head, upd, 0.0)

    for pr in range(N_PAIRS):
        kl = slice(pr * PAIR_LANES, (pr + 1) * PAIR_LANES)
        st = st_ref[pr]
        for c in range(n_chunks):
            t0 = c * CHUNK
            st = st * decays[c][:, kl] + upd_ref[c, pr]
            q = (p_ref[t0:t0 + CHUNK, OFF_Q + pr * PAIR_LANES:OFF_Q + (pr + 1) * PAIR_LANES]
                 * (B_HEAD_K ** -0.5)).astype(BF16)
            o = lax.dot_general(q, st.astype(BF16), (((1,), (1,)), ((), ())),
                                preferred_element_type=F32)
            p_ref[t0:t0 + CHUNK, OFF_VB + pr * PAIR_ROWS:OFF_VB + (pr + 1) * PAIR_ROWS] = o
        st_ref[pr] = st

    for hd in range(B_HEADS):
        cols = slice(OFF_VB + hd * B_HEAD_V, OFF_VB + (hd + 1) * B_HEAD_V)
        o = p_ref[:, cols]
        o = o * _rms_scale(o) * no_ref[:, hd * B_HEAD_V:(hd + 1) * B_HEAD_V]
        g = p_ref[:, OFF_G + hd * B_HEAD_V:OFF_G + (hd + 1) * B_HEAD_V]
        y_ref[:, A_WIDTH + hd * B_HEAD_V:A_WIDTH + (hd + 1) * B_HEAD_V] = (
            o * (g * _sigmoid(g))).astype(BF16)

    o_ref[...] = x_ref[...] + jnp.dot(y_ref[...], wout_ref[...], preferred_element_type=F32)


def mixer_call(x, norm_w, w_in_p, norm_v, w_s, b_s_full, w_gk_p, b_gk, norm_o, w_out):
    bsz, seq, _ = x.shape
    return pl.pallas_call(
        _mixer_kernel,
        out_shape=jax.ShapeDtypeStruct(x.shape, F32),
        grid=(bsz, seq // MIX_ROWS),
        in_specs=[
            pl.BlockSpec((None, MIX_ROWS, D_MODEL), lambda b, s: (b, s, 0)),
            _resident((1, D_MODEL)),
            _resident(w_in_p.shape),
            _resident((1, A_WIDTH)),
            _resident(w_s.shape),
            _resident(b_s_full.shape),
            _resident(w_gk_p.shape),
            _resident((1, B_KEY)),
            _resident((1, B_WIDTH)),
            _resident(w_out.shape),
        ],
        out_specs=pl.BlockSpec((None, MIX_ROWS, D_MODEL), lambda b, s: (b, s, 0)),
        scratch_shapes=[
            pltpu.VMEM((MIX_ROWS, IN_COLS_PAD), F32),
            pltpu.VMEM((MIX_ROWS, D_MODEL), BF16),
            pltpu.VMEM((MIX_ROWS // CHUNK, N_PAIRS, PAIR_ROWS, PAIR_LANES), F32),
            pltpu.VMEM((N_PAIRS, PAIR_ROWS, PAIR_LANES), F32),
        ],
        compiler_params=pltpu.CompilerParams(
            dimension_semantics=("arbitrary", "arbitrary"), vmem_limit_bytes=VMEM_LIMIT),
        name="mixer",
    )(x, norm_w, w_in_p, norm_v, w_s, b_s_full, w_gk_p, b_gk, norm_o, w_out)


def kernel(x, ff1_norm, ff1_w_in, ff1_w_out, mix_norm, w_in, gmlp_norm_v, gmlp_w_s, gmlp_b_s,
           gla_w_gk2, gla_b_gk, gla_norm_o, w_out, ff2_norm, ff2_w_in, ff2_w_out, final_norm):
    bsz, seq, d = x.shape
    depth = ff1_norm.shape[0]
    row = lambda v: v.reshape(1, -1)
    fw = row(final_norm)
    for l in range(depth):
        w1_in, w1_out = ff1_w_in[l].astype(BF16), ff1_w_out[l].astype(BF16)
        w2_in, w2_out = ff2_w_in[l].astype(BF16), ff2_w_out[l].astype(BF16)
        w_in_p = jnp.pad(w_in[l], ((0, 0), (0, R_PAD - GATE_RANK))).astype(BF16)
        w_gk_p = jnp.pad(gla_w_gk2[l], ((0, R_PAD - GATE_RANK), (0, 0))).astype(BF16)
        b_s_full = jnp.repeat(gmlp_b_s[l].T, A_GROUP_DIM, axis=1)

        x = ffn_call(x.reshape(bsz * seq, d), row(ff1_norm[l]), w1_in, w1_out, fw,
                     final_norm=False).reshape(bsz, seq, d)
        x = mixer_call(x, row(mix_norm[l]), w_in_p, row(gmlp_norm_v[l]), gmlp_w_s[l], b_s_full,
                       w_gk_p, row(gla_b_gk[l]), row(gla_norm_o[l]), w_out[l].astype(BF16))
        x = ffn_call(x.reshape(bsz * seq, d), row(ff2_norm[l]), w2_in, w2_out, fw,
                     final_norm=(l == depth - 1)).reshape(bsz, seq, d)
    return x
```

```python
import functools

import jax
import jax.numpy as jnp
from jax import lax
from jax.experimental import pallas as pl
from jax.experimental.pallas import tpu as pltpu

D_MODEL = 1024
D_FF = 2816
CHUNK = 64
A_WIDTH = 512
A_GROUPS = 8
A_GROUP_DIM = A_WIDTH // A_GROUPS
A_BLOCK = 128
B_WIDTH = 512
B_HEADS = 4
B_HEAD_V = 128
B_HEAD_K = 64
B_KEY = B_HEADS * B_HEAD_K
GATE_RANK = 16
GATE_NORM = 16.0
EPS = 1e-6

LANES = 128
R_PAD = LANES
OFF_R = 0
OFF_Q = OFF_R + R_PAD
OFF_K = OFF_Q + B_KEY
OFF_VB = OFF_K + B_KEY
OFF_U = OFF_VB + B_WIDTH
OFF_VA = OFF_U + A_WIDTH
OFF_G = OFF_VA + A_WIDTH
IN_COLS_PAD = OFF_G + B_WIDTH

FF_CHUNK = 256
N_FF_CHUNKS = D_FF // FF_CHUNK
FFN_ROWS = 1024
MIX_ROWS = 512
VMEM_LIMIT = 56 * 1024 * 1024

PAIR_ROWS = 2 * B_HEAD_V
PAIR_LANES = 2 * B_HEAD_K
N_PAIRS = B_HEADS // 2

BF16 = jnp.bfloat16
F32 = jnp.float32


def _rms_scale(x):
    return lax.rsqrt(jnp.mean(x * x, axis=-1, keepdims=True) + EPS)


def _sigmoid(x):
    return 1.0 / (1.0 + jnp.exp(-x))


def _split3(x):
    hi = x.astype(BF16)
    r1 = x - hi.astype(F32)
    mid = r1.astype(BF16)
    lo = (r1 - mid.astype(F32)).astype(BF16)
    return hi, mid, lo


def _ffn_kernel(x_ref, nw_ref, win_ref, wout_ref, fw_ref, o_ref, xn_ref, acc_ref, *, final_norm):
    x = x_ref[...]
    xn_ref[...] = (x * _rms_scale(x) * nw_ref[...]).astype(BF16)
    for c in range(N_FF_CHUNKS):
        lo = c * FF_CHUNK
        a = jnp.dot(xn_ref[...], win_ref[:, lo:lo + FF_CHUNK], preferred_element_type=F32)
        g = jnp.dot(xn_ref[...], win_ref[:, D_FF + lo:D_FF + lo + FF_CHUNK],
                    preferred_element_type=F32)
        hh = (g * _sigmoid(g) * a).astype(BF16)
        part = jnp.dot(hh, wout_ref[lo:lo + FF_CHUNK, :], preferred_element_type=F32)
        if c == 0:
            acc_ref[...] = part
        elif c < N_FF_CHUNKS - 1:
            acc_ref[...] += part
        else:
            y = x_ref[...] + 0.5 * (acc_ref[...] + part)
            if final_norm:
                y = y * _rms_scale(y) * fw_ref[...]
            o_ref[...] = y


def _resident(shape):
    zeros = (0,) * len(shape)
    return pl.BlockSpec(shape, lambda *_: zeros, pipeline_mode=pl.Buffered(1))


def ffn_call(x2d, norm_w, w_in_c, w_out_c, final_w, *, final_norm):
    n = x2d.shape[0]
    return pl.pallas_call(
        functools.partial(_ffn_kernel, final_norm=final_norm),
        out_shape=jax.ShapeDtypeStruct(x2d.shape, F32),
        grid=(n // FFN_ROWS,),
        in_specs=[
            pl.BlockSpec((FFN_ROWS, D_MODEL), lambda i: (i, 0)),
            _resident((1, D_MODEL)),
            _resident(w_in_c.shape),
            _resident(w_out_c.shape),
            _resident((1, D_MODEL)),
        ],
        out_specs=pl.BlockSpec((FFN_ROWS, D_MODEL), lambda i: (i, 0)),
        scratch_shapes=[
            pltpu.VMEM((FFN_ROWS, D_MODEL), BF16),
            pltpu.VMEM((FFN_ROWS, D_MODEL), F32),
        ],
        compiler_params=pltpu.CompilerParams(
            dimension_semantics=("arbitrary",), vmem_limit_bytes=VMEM_LIMIT),
        name="ffn",
    )(x2d, norm_w, w_in_c, w_out_c, final_w)


def _mixer_kernel(x_ref, nw_ref, win_ref, nv_ref, ws_ref, bs_ref, wgk_ref, bgk_ref, no_ref,
                  wout_ref, o_ref, h_ref, p_ref, y_ref, upd_ref, stb_ref, st_ref):
    rows = x_ref.shape[0]
    n_chunks = rows // CHUNK

    @pl.when(pl.program_id(1) == 0)
    def _():
        st_ref[...] = jnp.zeros_like(st_ref)

    def project(lo, hi):
        return jnp.dot(h_ref[...], win_ref[:, lo:hi], preferred_element_type=F32)

    x = x_ref[...]
    h_ref[...] = (x * _rms_scale(x) * nw_ref[...]).astype(BF16)

    p_ref[:, OFF_R:OFF_U] = project(OFF_R, OFF_U)
    r = p_ref[:, OFF_R:OFF_R + R_PAD].astype(BF16)
    gate_pre = jnp.dot(r, wgk_ref[...], preferred_element_type=F32) + bgk_ref[...]
    p_ref[:, OFF_U:OFF_G] = project(OFF_U, OFF_G)
    log_a = (jnp.minimum(gate_pre, 0.0) - jnp.log1p(jnp.exp(-jnp.abs(gate_pre)))) / GATE_NORM

    tri = (lax.broadcasted_iota(jnp.int32, (CHUNK, CHUNK), 0)
           >= lax.broadcasted_iota(jnp.int32, (CHUNK, CHUNK), 1)).astype(BF16)
    cum3 = [jnp.dot(tri, jnp.concatenate(_split3(log_a[c * CHUNK:(c + 1) * CHUNK, :]), axis=1),
                    preferred_element_type=F32) for c in range(n_chunks)]

    p_ref[:, OFF_G:IN_COLS_PAD] = project(OFF_G, IN_COLS_PAD)

    same_head = (lax.broadcasted_iota(jnp.int32, (PAIR_ROWS, PAIR_LANES), 0) // B_HEAD_V
                 == lax.broadcasted_iota(jnp.int32, (PAIR_ROWS, PAIR_LANES), 1) // B_HEAD_K)
    decays = []
    for c in range(n_chunks):
        t0 = c * CHUNK
        cum = cum3[c][:, :B_KEY] + cum3[c][:, B_KEY:2 * B_KEY] + cum3[c][:, 2 * B_KEY:]
        tot = cum[CHUNK - 1:CHUNK, :]
        decays.append(jnp.exp(tot))
        k_dec = (p_ref[t0:t0 + CHUNK, OFF_K:OFF_K + B_KEY] * jnp.exp(tot - cum)).astype(BF16)
        v = p_ref[t0:t0 + CHUNK, OFF_VB:OFF_VB + B_WIDTH].astype(BF16)
        for pr in range(N_PAIRS):
            kl = slice(pr * PAIR_LANES, (pr + 1) * PAIR_LANES)
            vl = slice(pr * PAIR_ROWS, (pr + 1) * PAIR_ROWS)
            upd = lax.dot_general(v[:, vl], k_dec[:, kl], (((0,), (0,)), ((), ())),
                                  preferred_element_type=F32)
            upd_ref[c, pr] = jnp.where(same_head, upd, 0.0)

    va = p_ref[:, OFF_VA:OFF_VA + A_WIDTH]
    vn = (va * _rms_scale(va) * nv_ref[...]).astype(BF16)
    t_chunk = lax.broadcasted_iota(jnp.int32, (A_BLOCK, A_BLOCK), 0) // CHUNK
    s_chunk = lax.broadcasted_iota(jnp.int32, (A_BLOCK, A_BLOCK), 1) // CHUNK
    causal = t_chunk >= s_chunk
    half_groups = A_GROUPS // 2
    half_width = A_WIDTH // 2
    ws_half = [
        jnp.concatenate([jnp.where(causal, ws_ref[hf * half_groups + g], 0.0).astype(BF16)
                         for g in range(half_groups)], axis=1)
        for hf in range(2)]
    lane_group = lax.broadcasted_iota(jnp.int32, (A_BLOCK, half_width), 1) // A_GROUP_DIM
    for blk in range(rows // A_BLOCK):
        r0 = blk * A_BLOCK
        for hf in range(2):
            c0 = hf * half_width
            v_blk = vn[r0:r0 + A_BLOCK, c0:c0 + half_width]
            v_stack = jnp.concatenate(
                [jnp.where(lane_group == g, v_blk, jnp.zeros_like(v_blk))
                 for g in range(half_groups)], axis=0)
            z = (jnp.dot(ws_half[hf], v_stack, preferred_element_type=F32)
                 + bs_ref[:, c0:c0 + half_width])
            u = p_ref[r0:r0 + A_BLOCK, OFF_U + c0:OFF_U + c0 + half_width]
            y_ref[r0:r0 + A_BLOCK, c0:c0 + half_width] = (u * z).astype(BF16)

    for pr in range(N_PAIRS):
        kl = slice(pr * PAIR_LANES, (pr + 1) * PAIR_LANES)
        st = st_ref[pr]
        for c in range(n_chunks):
            st = st * decays[c][:, kl] + upd_ref[c, pr]
            stb_ref[c, pr] = st.astype(BF16)
        st_ref[pr] = st

    for c in range(n_chunks):
        t0 = c * CHUNK
        for pr in range(N_PAIRS):
            q = (p_ref[t0:t0 + CHUNK, OFF_Q + pr * PAIR_LANES:OFF_Q + (pr + 1) * PAIR_LANES]
                 * (B_HEAD_K ** -0.5)).astype(BF16)
            o = lax.dot_general(q, stb_ref[c, pr], (((1,), (1,)), ((), ())),
                                preferred_element_type=F32)
            p_ref[t0:t0 + CHUNK, OFF_VB + pr * PAIR_ROWS:OFF_VB + (pr + 1) * PAIR_ROWS] = o

    out_a = jnp.dot(y_ref[:, :A_WIDTH], wout_ref[:A_WIDTH, :], preferred_element_type=F32)

    for hd in range(B_HEADS):
        cols = slice(hd * B_HEAD_V, (hd + 1) * B_HEAD_V)
        o = p_ref[:, OFF_VB + hd * B_HEAD_V:OFF_VB + (hd + 1) * B_HEAD_V]
        o = o * _rms_scale(o) * no_ref[:, cols]
        g = p_ref[:, OFF_G + hd * B_HEAD_V:OFF_G + (hd + 1) * B_HEAD_V]
        y_ref[:, A_WIDTH + hd * B_HEAD_V:A_WIDTH + (hd + 1) * B_HEAD_V] = (
            o * (g * _sigmoid(g))).astype(BF16)

    out_b = jnp.dot(y_ref[:, A_WIDTH:], wout_ref[A_WIDTH:, :], preferred_element_type=F32)
    o_ref[...] = x_ref[...] + (out_a + out_b)


def mixer_call(x, norm_w, w_in_p, norm_v, w_s, b_s_full, w_gk_p, b_gk, norm_o, w_out):
    bsz, seq, _ = x.shape
    return pl.pallas_call(
        _mixer_kernel,
        out_shape=jax.ShapeDtypeStruct(x.shape, F32),
        grid=(bsz, seq // MIX_ROWS),
        in_specs=[
            pl.BlockSpec((None, MIX_ROWS, D_MODEL), lambda b, s: (b, s, 0)),
            _resident((1, D_MODEL)),
            _resident(w_in_p.shape),
            _resident((1, A_WIDTH)),
            _resident(w_s.shape),
            _resident(b_s_full.shape),
            _resident(w_gk_p.shape),
            _resident((1, B_KEY)),
            _resident((1, B_WIDTH)),
            _resident(w_out.shape),
        ],
        out_specs=pl.BlockSpec((None, MIX_ROWS, D_MODEL), lambda b, s: (b, s, 0)),
        scratch_shapes=[
            pltpu.VMEM((MIX_ROWS, D_MODEL), BF16),
            pltpu.VMEM((MIX_ROWS, IN_COLS_PAD), F32),
            pltpu.VMEM((MIX_ROWS, D_MODEL), BF16),
            pltpu.VMEM((MIX_ROWS // CHUNK, N_PAIRS, PAIR_ROWS, PAIR_LANES), F32),
            pltpu.VMEM((MIX_ROWS // CHUNK, N_PAIRS, PAIR_ROWS, PAIR_LANES), BF16),
            pltpu.VMEM((N_PAIRS, PAIR_ROWS, PAIR_LANES), F32),
        ],
        compiler_params=pltpu.CompilerParams(
            dimension_semantics=("arbitrary", "arbitrary"), vmem_limit_bytes=VMEM_LIMIT),
        name="mixer",
    )(x, norm_w, w_in_p, norm_v, w_s, b_s_full, w_gk_p, b_gk, norm_o, w_out)


def _regroup_w_in(w):
    a2, bk, bw = 2 * A_WIDTH, B_KEY, B_WIDTH
    qkv = w[:, a2:a2 + 2 * bk + bw]
    g = w[:, a2 + 2 * bk + bw:a2 + 2 * bk + 2 * bw]
    r = jnp.pad(w[:, a2 + 2 * bk + 2 * bw:], ((0, 0), (0, R_PAD - GATE_RANK)))
    return jnp.concatenate([r, qkv, w[:, :a2], g], axis=1).astype(BF16)


def kernel(x, ff1_norm, ff1_w_in, ff1_w_out, mix_norm, w_in, gmlp_norm_v, gmlp_w_s, gmlp_b_s,
           gla_w_gk2, gla_b_gk, gla_norm_o, w_out, ff2_norm, ff2_w_in, ff2_w_out, final_norm):
    bsz, seq, d = x.shape
    depth = ff1_norm.shape[0]
    row = lambda v: v.reshape(1, -1)
    fw = row(final_norm)
    for l in range(depth):
        w1_in, w1_out = ff1_w_in[l].astype(BF16), ff1_w_out[l].astype(BF16)
        w2_in, w2_out = ff2_w_in[l].astype(BF16), ff2_w_out[l].astype(BF16)
        w_in_p = _regroup_w_in(w_in[l])
        w_gk_p = jnp.pad(gla_w_gk2[l], ((0, R_PAD - GATE_RANK), (0, 0))).astype(BF16)
        b_s_full = jnp.repeat(gmlp_b_s[l].T, A_GROUP_DIM, axis=1)

        x = ffn_call(x.reshape(bsz * seq, d), row(ff1_norm[l]), w1_in, w1_out, fw,
                     final_norm=False).reshape(bsz, seq, d)
        x = mixer_call(x, row(mix_norm[l]), w_in_p, row(gmlp_norm_v[l]), gmlp_w_s[l], b_s_full,
                       w_gk_p, row(gla_b_gk[l]), row(gla_norm_o[l]), w_out[l].astype(BF16))
        x = ffn_call(x.reshape(bsz * seq, d), row(ff2_norm[l]), w2_in, w2_out, fw,
                     final_norm=(l == depth - 1)).reshape(bsz, seq, d)
    return x
```

```python
import functools

import jax
import jax.numpy as jnp
from jax import lax
from jax.experimental import pallas as pl
from jax.experimental.pallas import tpu as pltpu

D_MODEL = 1024
D_FF = 2816
CHUNK = 64
A_WIDTH = 512
A_GROUPS = 8
A_GROUP_DIM = A_WIDTH // A_GROUPS
A_BLOCK = 128
B_WIDTH = 512
B_HEADS = 4
B_HEAD_V = 128
B_HEAD_K = 64
B_KEY = B_HEADS * B_HEAD_K
GATE_RANK = 16
GATE_NORM = 16.0
EPS = 1e-6

LANES = 128
R_PAD = LANES
OFF_R = 0
OFF_Q = OFF_R + R_PAD
OFF_K = OFF_Q + B_KEY
OFF_VB = OFF_K + B_KEY
OFF_U = OFF_VB + B_WIDTH
OFF_VA = OFF_U + A_WIDTH
OFF_G = OFF_VA + A_WIDTH
IN_COLS_PAD = OFF_G + B_WIDTH

FF_CHUNK = 256
N_FF_CHUNKS = D_FF // FF_CHUNK
FFN_ROWS = 512
MIX_ROWS = 512
VMEM_LIMIT = 56 * 1024 * 1024

PAIR_ROWS = 2 * B_HEAD_V
PAIR_LANES = 2 * B_HEAD_K
N_PAIRS = B_HEADS // 2

BF16 = jnp.bfloat16
F32 = jnp.float32


def _rms_scale(x):
    return lax.rsqrt(jnp.mean(x * x, axis=-1, keepdims=True) + EPS)


def _sigmoid(x):
    return 1.0 / (1.0 + jnp.exp(-x))


def _split3(x):
    hi = x.astype(BF16)
    r1 = x - hi.astype(F32)
    mid = r1.astype(BF16)
    lo = (r1 - mid.astype(F32)).astype(BF16)
    return hi, mid, lo


def _ffn_kernel(x_ref, nw_ref, win_ref, wout_ref, fw_ref, o_ref, xn_ref, acc_ref, *, final_norm):
    x = x_ref[...]
    xn_ref[...] = (x * nw_ref[...]).astype(BF16)
    scale = _rms_scale(x)

    def up(c):
        lo = c * FF_CHUNK
        xn = xn_ref[...]
        a = jnp.dot(xn, win_ref[:, lo:lo + FF_CHUNK].astype(BF16), preferred_element_type=F32)
        g = jnp.dot(xn, win_ref[:, D_FF + lo:D_FF + lo + FF_CHUNK].astype(BF16),
                    preferred_element_type=F32)
        return a, g

    nxt = up(0)
    for c in range(N_FF_CHUNKS):
        lo = c * FF_CHUNK
        a, g = nxt
        if c + 1 < N_FF_CHUNKS:
            nxt = up(c + 1)
        g = scale * g
        hh = (g * _sigmoid(g) * (scale * a)).astype(BF16)
        part = jnp.dot(hh, wout_ref[lo:lo + FF_CHUNK, :].astype(BF16),
                       preferred_element_type=F32)
        if c == 0:
            acc_ref[...] = part
        elif c < N_FF_CHUNKS - 1:
            acc_ref[...] += part
        else:
            y = x_ref[...] + 0.5 * (acc_ref[...] + part)
            if final_norm:
                y = y * _rms_scale(y) * fw_ref[...]
            o_ref[...] = y


def _resident(shape):
    zeros = (0,) * len(shape)
    return pl.BlockSpec(shape, lambda *_: zeros, pipeline_mode=pl.Buffered(1))


def _layer_resident(shape, layer):
    index = (layer,) + (0,) * (len(shape) - 1)
    return pl.BlockSpec((None,) + tuple(shape[1:]), lambda *_: index,
                        pipeline_mode=pl.Buffered(1))


def ffn_call(x2d, norm_w, w_in, w_out, final_w, *, layer, final_norm):
    n = x2d.shape[0]
    return pl.pallas_call(
        functools.partial(_ffn_kernel, final_norm=final_norm),
        out_shape=jax.ShapeDtypeStruct(x2d.shape, F32),
        grid=(n // FFN_ROWS,),
        in_specs=[
            pl.BlockSpec((FFN_ROWS, D_MODEL), lambda i: (i, 0)),
            _layer_resident(norm_w.shape, layer),
            _layer_resident(w_in.shape, layer),
            _layer_resident(w_out.shape, layer),
            _resident((1, D_MODEL)),
        ],
        out_specs=pl.BlockSpec((FFN_ROWS, D_MODEL), lambda i: (i, 0)),
        scratch_shapes=[
            pltpu.VMEM((FFN_ROWS, D_MODEL), BF16),
            pltpu.VMEM((FFN_ROWS, D_MODEL), F32),
        ],
        compiler_params=pltpu.CompilerParams(
            dimension_semantics=("arbitrary",), vmem_limit_bytes=VMEM_LIMIT),
        name="ffn",
    )(x2d, norm_w, w_in, w_out, final_w)


def _mixer_kernel(x_ref, nw_ref, win_ref, nv_ref, ws_ref, bs_ref, wgk_ref, bgk_ref, no_ref,
                  wout_ref, o_ref, h_ref, p_ref, y_ref, upd_ref, stb_ref, st_ref):
    rows = x_ref.shape[0]
    n_chunks = rows // CHUNK

    @pl.when(pl.program_id(1) == 0)
    def _():
        st_ref[...] = jnp.zeros_like(st_ref)

    def project(lo, hi):
        return jnp.dot(h_ref[...], win_ref[:, lo:hi], preferred_element_type=F32)

    x = x_ref[...]
    h_ref[...] = (x * _rms_scale(x) * nw_ref[...]).astype(BF16)

    p_ref[:, OFF_R:OFF_U] = project(OFF_R, OFF_U)
    r = p_ref[:, OFF_R:OFF_R + R_PAD].astype(BF16)
    gate_pre = jnp.dot(r, wgk_ref[...], preferred_element_type=F32) + bgk_ref[...]
    p_ref[:, OFF_U:OFF_G] = project(OFF_U, OFF_G)
    log_a = (jnp.minimum(gate_pre, 0.0) - jnp.log1p(jnp.exp(-jnp.abs(gate_pre)))) / GATE_NORM

    tri = (lax.broadcasted_iota(jnp.int32, (CHUNK, CHUNK), 0)
           >= lax.broadcasted_iota(jnp.int32, (CHUNK, CHUNK), 1)).astype(BF16)
    cum3 = [jnp.dot(tri, jnp.concatenate(_split3(log_a[c * CHUNK:(c + 1) * CHUNK, :]), axis=1),
                    preferred_element_type=F32) for c in range(n_chunks)]

    p_ref[:, OFF_G:IN_COLS_PAD] = project(OFF_G, IN_COLS_PAD)

    same_head = (lax.broadcasted_iota(jnp.int32, (PAIR_ROWS, PAIR_LANES), 0) // B_HEAD_V
                 == lax.broadcasted_iota(jnp.int32, (PAIR_ROWS, PAIR_LANES), 1) // B_HEAD_K)
    decays = []
    for c in range(n_chunks):
        t0 = c * CHUNK
        cum = cum3[c][:, :B_KEY] + cum3[c][:, B_KEY:2 * B_KEY] + cum3[c][:, 2 * B_KEY:]
        tot = cum[CHUNK - 1:CHUNK, :]
        decays.append(jnp.exp(tot))
        k_dec = (p_ref[t0:t0 + CHUNK, OFF_K:OFF_K + B_KEY] * jnp.exp(tot - cum)).astype(BF16)
        v = p_ref[t0:t0 + CHUNK, OFF_VB:OFF_VB + B_WIDTH].astype(BF16)
        for pr in range(N_PAIRS):
            kl = slice(pr * PAIR_LANES, (pr + 1) * PAIR_LANES)
            vl = slice(pr * PAIR_ROWS, (pr + 1) * PAIR_ROWS)
            upd = lax.dot_general(v[:, vl], k_dec[:, kl], (((0,), (0,)), ((), ())),
                                  preferred_element_type=F32)
            upd_ref[c, pr] = jnp.where(same_head, upd, 0.0)

    va = p_ref[:, OFF_VA:OFF_VA + A_WIDTH]
    vn = (va * _rms_scale(va) * nv_ref[...]).astype(BF16)
    t_chunk = lax.broadcasted_iota(jnp.int32, (A_BLOCK, A_BLOCK), 0) // CHUNK
    s_chunk = lax.broadcasted_iota(jnp.int32, (A_BLOCK, A_BLOCK), 1) // CHUNK
    causal = t_chunk >= s_chunk
    half_groups = A_GROUPS // 2
    half_width = A_WIDTH // 2
    ws_half = [
        jnp.concatenate([jnp.where(causal, ws_ref[hf * half_groups + g], 0.0).astype(BF16)
                         for g in range(half_groups)], axis=1)
        for hf in range(2)]
    lane_group = lax.broadcasted_iota(jnp.int32, (A_BLOCK, half_width), 1) // A_GROUP_DIM
    for blk in range(rows // A_BLOCK):
        r0 = blk * A_BLOCK
        for hf in range(2):
            c0 = hf * half_width
            v_blk = vn[r0:r0 + A_BLOCK, c0:c0 + half_width]
            v_stack = jnp.concatenate(
                [jnp.where(lane_group == g, v_blk, jnp.zeros_like(v_blk))
                 for g in range(half_groups)], axis=0)
            z = (jnp.dot(ws_half[hf], v_stack, preferred_element_type=F32)
                 + bs_ref[:, c0:c0 + half_width])
            u = p_ref[r0:r0 + A_BLOCK, OFF_U + c0:OFF_U + c0 + half_width]
            y_ref[r0:r0 + A_BLOCK, c0:c0 + half_width] = (u * z).astype(BF16)

    for pr in range(N_PAIRS):
        kl = slice(pr * PAIR_LANES, (pr + 1) * PAIR_LANES)
        st = st_ref[pr]
        for c in range(n_chunks):
            st = st * decays[c][:, kl] + upd_ref[c, pr]
            stb_ref[c, pr] = st.astype(BF16)
        st_ref[pr] = st

    for c in range(n_chunks):
        t0 = c * CHUNK
        for pr in range(N_PAIRS):
            q = (p_ref[t0:t0 + CHUNK, OFF_Q + pr * PAIR_LANES:OFF_Q + (pr + 1) * PAIR_LANES]
                 * (B_HEAD_K ** -0.5)).astype(BF16)
            o = lax.dot_general(q, stb_ref[c, pr], (((1,), (1,)), ((), ())),
                                preferred_element_type=F32)
            p_ref[t0:t0 + CHUNK, OFF_VB + pr * PAIR_ROWS:OFF_VB + (pr + 1) * PAIR_ROWS] = o

    out_a = jnp.dot(y_ref[:, :A_WIDTH], wout_ref[:A_WIDTH, :], preferred_element_type=F32)

    for hd in range(B_HEADS):
        cols = slice(hd * B_HEAD_V, (hd + 1) * B_HEAD_V)
        o = p_ref[:, OFF_VB + hd * B_HEAD_V:OFF_VB + (hd + 1) * B_HEAD_V]
        o = o * _rms_scale(o) * no_ref[:, cols]
        g = p_ref[:, OFF_G + hd * B_HEAD_V:OFF_G + (hd + 1) * B_HEAD_V]
        y_ref[:, A_WIDTH + hd * B_HEAD_V:A_WIDTH + (hd + 1) * B_HEAD_V] = (
            o * (g * _sigmoid(g))).astype(BF16)

    out_b = jnp.dot(y_ref[:, A_WIDTH:], wout_ref[A_WIDTH:, :], preferred_element_type=F32)
    o_ref[...] = x_ref[...] + (out_a + out_b)


def mixer_call(x, norm_w, w_in_p, norm_v, w_s, b_s_full, w_gk_p, b_gk, norm_o, w_out):
    bsz, seq, _ = x.shape
    return pl.pallas_call(
        _mixer_kernel,
        out_shape=jax.ShapeDtypeStruct(x.shape, F32),
        grid=(bsz, seq // MIX_ROWS),
        in_specs=[
            pl.BlockSpec((None, MIX_ROWS, D_MODEL), lambda b, s: (b, s, 0)),
            _resident((1, D_MODEL)),
            _resident(w_in_p.shape),
            _resident((1, A_WIDTH)),
            _resident(w_s.shape),
            _resident(b_s_full.shape),
            _resident(w_gk_p.shape),
            _resident((1, B_KEY)),
            _resident((1, B_WIDTH)),
            _resident(w_out.shape),
        ],
        out_specs=pl.BlockSpec((None, MIX_ROWS, D_MODEL), lambda b, s: (b, s, 0)),
        scratch_shapes=[
            pltpu.VMEM((MIX_ROWS, D_MODEL), BF16),
            pltpu.VMEM((MIX_ROWS, IN_COLS_PAD), F32),
            pltpu.VMEM((MIX_ROWS, D_MODEL), BF16),
            pltpu.VMEM((MIX_ROWS // CHUNK, N_PAIRS, PAIR_ROWS, PAIR_LANES), F32),
            pltpu.VMEM((MIX_ROWS // CHUNK, N_PAIRS, PAIR_ROWS, PAIR_LANES), BF16),
            pltpu.VMEM((N_PAIRS, PAIR_ROWS, PAIR_LANES), F32),
        ],
        compiler_params=pltpu.CompilerParams(
            dimension_semantics=("arbitrary", "arbitrary"), vmem_limit_bytes=VMEM_LIMIT),
        name="mixer",
    )(x, norm_w, w_in_p, norm_v, w_s, b_s_full, w_gk_p, b_gk, norm_o, w_out)


def _regroup_w_in(w):
    a2, bk, bw = 2 * A_WIDTH, B_KEY, B_WIDTH
    qkv = w[:, a2:a2 + 2 * bk + bw]
    g = w[:, a2 + 2 * bk + bw:a2 + 2 * bk + 2 * bw]
    r = jnp.pad(w[:, a2 + 2 * bk + 2 * bw:], ((0, 0), (0, R_PAD - GATE_RANK)))
    return jnp.concatenate([r, qkv, w[:, :a2], g], axis=1).astype(BF16)


def kernel(x, ff1_norm, ff1_w_in, ff1_w_out, mix_norm, w_in, gmlp_norm_v, gmlp_w_s, gmlp_b_s,
           gla_w_gk2, gla_b_gk, gla_norm_o, w_out, ff2_norm, ff2_w_in, ff2_w_out, final_norm):
    bsz, seq, d = x.shape
    depth = ff1_norm.shape[0]
    row = lambda v: v.reshape(1, -1)
    rows3 = lambda v: v.reshape(depth, 1, -1)
    fw = row(final_norm)
    ff1_nw, ff2_nw = rows3(ff1_norm), rows3(ff2_norm)
    for l in range(depth):
        w_in_p = _regroup_w_in(w_in[l])
        w_gk_p = jnp.pad(gla_w_gk2[l], ((0, R_PAD - GATE_RANK), (0, 0))).astype(BF16)
        b_s_full = jnp.repeat(gmlp_b_s[l].T, A_GROUP_DIM, axis=1)

        x = ffn_call(x.reshape(bsz * seq, d), ff1_nw, ff1_w_in, ff1_w_out, fw,
                     layer=l, final_norm=False).reshape(bsz, seq, d)
        x = mixer_call(x, row(mix_norm[l]), w_in_p, row(gmlp_norm_v[l]), gmlp_w_s[l], b_s_full,
                       w_gk_p, row(gla_b_gk[l]), row(gla_norm_o[l]), w_out[l].astype(BF16))
        x = ffn_call(x.reshape(bsz * seq, d), ff2_nw, ff2_w_in, ff2_w_out, fw,
                     layer=l, final_norm=(l == depth - 1)).reshape(bsz, seq, d)
    return x
```

```python
import functools

import jax
import jax.numpy as jnp
from jax import lax
from jax.experimental import pallas as pl
from jax.experimental.pallas import tpu as pltpu

D_MODEL = 1024
D_FF = 2816
CHUNK = 64
A_WIDTH = 512
A_GROUPS = 8
A_GROUP_DIM = A_WIDTH // A_GROUPS
A_BLOCK = 128
B_WIDTH = 512
B_HEADS = 4
B_HEAD_V = 128
B_HEAD_K = 64
B_KEY = B_HEADS * B_HEAD_K
GATE_RANK = 16
GATE_NORM = 16.0
EPS = 1e-6

LANES = 128
R_PAD = LANES
OFF_R = 0
OFF_Q = OFF_R + R_PAD
OFF_K = OFF_Q + B_KEY
OFF_VB = OFF_K + B_KEY
OFF_U = OFF_VB + B_WIDTH
OFF_VA = OFF_U + A_WIDTH
OFF_G = OFF_VA + A_WIDTH
IN_COLS_PAD = OFF_G + B_WIDTH

FF_CHUNK = 256
N_FF_CHUNKS = D_FF // FF_CHUNK
FFN_ROWS = 512
MIX_ROWS = 512
MIX_SEQS = 2
VMEM_LIMIT = 56 * 1024 * 1024

PAIR_ROWS = 2 * B_HEAD_V
PAIR_LANES = 2 * B_HEAD_K
N_PAIRS = B_HEADS // 2

BF16 = jnp.bfloat16
F32 = jnp.float32


def _rms_scale(x):
    return lax.rsqrt(jnp.mean(x * x, axis=-1, keepdims=True) + EPS)


def _sigmoid(x):
    return 1.0 / (1.0 + jnp.exp(-x))


def _split3(x):
    hi = x.astype(BF16)
    r1 = x - hi.astype(F32)
    mid = r1.astype(BF16)
    lo = (r1 - mid.astype(F32)).astype(BF16)
    return hi, mid, lo


def _ffn_kernel(x_ref, nw_ref, win_ref, wout_ref, fw_ref, o_ref, xn_ref, acc_ref, *, final_norm):
    x = x_ref[...]
    xn_ref[...] = (x * nw_ref[...]).astype(BF16)
    scale = _rms_scale(x)

    def up(c):
        lo = c * FF_CHUNK
        xn = xn_ref[...]
        a = jnp.dot(xn, win_ref[:, lo:lo + FF_CHUNK].astype(BF16), preferred_element_type=F32)
        g = jnp.dot(xn, win_ref[:, D_FF + lo:D_FF + lo + FF_CHUNK].astype(BF16),
                    preferred_element_type=F32)
        return a, g

    nxt = up(0)
    for c in range(N_FF_CHUNKS):
        lo = c * FF_CHUNK
        a, g = nxt
        if c + 1 < N_FF_CHUNKS:
            nxt = up(c + 1)
        g = scale * g
        hh = (g * _sigmoid(g) * (scale * a)).astype(BF16)
        part = jnp.dot(hh, wout_ref[lo:lo + FF_CHUNK, :].astype(BF16),
                       preferred_element_type=F32)
        if c == 0:
            acc_ref[...] = part
        elif c < N_FF_CHUNKS - 1:
            acc_ref[...] += part
        else:
            y = x_ref[...] + 0.5 * (acc_ref[...] + part)
            if final_norm:
                y = y * _rms_scale(y) * fw_ref[...]
            o_ref[...] = y


def _resident(shape):
    zeros = (0,) * len(shape)
    return pl.BlockSpec(shape, lambda *_: zeros, pipeline_mode=pl.Buffered(1))


def _layer_resident(shape, layer):
    index = (layer,) + (0,) * (len(shape) - 1)
    return pl.BlockSpec((None,) + tuple(shape[1:]), lambda *_: index,
                        pipeline_mode=pl.Buffered(1))


def ffn_call(x2d, norm_w, w_in, w_out, final_w, *, layer, final_norm):
    n = x2d.shape[0]
    return pl.pallas_call(
        functools.partial(_ffn_kernel, final_norm=final_norm),
        out_shape=jax.ShapeDtypeStruct(x2d.shape, F32),
        grid=(n // FFN_ROWS,),
        in_specs=[
            pl.BlockSpec((FFN_ROWS, D_MODEL), lambda i: (i, 0)),
            _layer_resident(norm_w.shape, layer),
            _layer_resident(w_in.shape, layer),
            _layer_resident(w_out.shape, layer),
            _resident((1, D_MODEL)),
        ],
        out_specs=pl.BlockSpec((FFN_ROWS, D_MODEL), lambda i: (i, 0)),
        scratch_shapes=[
            pltpu.VMEM((FFN_ROWS, D_MODEL), BF16),
            pltpu.VMEM((FFN_ROWS, D_MODEL), F32),
        ],
        compiler_params=pltpu.CompilerParams(
            dimension_semantics=("arbitrary",), vmem_limit_bytes=VMEM_LIMIT),
        name="ffn",
    )(x2d, norm_w, w_in, w_out, final_w)


def _mixer_kernel(x_ref, nw_ref, win_ref, nv_ref, ws_ref, bs_ref, wgk_ref, bgk_ref, no_ref,
                  wout_ref, o_ref, h_ref, p_ref, y_ref, upd_ref, stb_ref, st_ref):
    n_seqs, rows = x_ref.shape[0], x_ref.shape[1]
    n_chunks = rows // CHUNK

    @pl.when(pl.program_id(1) == 0)
    def _():
        st_ref[...] = jnp.zeros_like(st_ref)

    tri = (lax.broadcasted_iota(jnp.int32, (CHUNK, CHUNK), 0)
           >= lax.broadcasted_iota(jnp.int32, (CHUNK, CHUNK), 1)).astype(BF16)
    same_head = (lax.broadcasted_iota(jnp.int32, (PAIR_ROWS, PAIR_LANES), 0) // B_HEAD_V
                 == lax.broadcasted_iota(jnp.int32, (PAIR_ROWS, PAIR_LANES), 1) // B_HEAD_K)
    t_chunk = lax.broadcasted_iota(jnp.int32, (A_BLOCK, A_BLOCK), 0) // CHUNK
    s_chunk = lax.broadcasted_iota(jnp.int32, (A_BLOCK, A_BLOCK), 1) // CHUNK
    causal = t_chunk >= s_chunk
    half_groups = A_GROUPS // 2
    half_width = A_WIDTH // 2
    ws_half = [
        jnp.concatenate([jnp.where(causal, ws_ref[hf * half_groups + g], 0.0).astype(BF16)
                         for g in range(half_groups)], axis=1)
        for hf in range(2)]
    lane_group = lax.broadcasted_iota(jnp.int32, (A_BLOCK, half_width), 1) // A_GROUP_DIM

    ctx = [dict() for _ in range(n_seqs)]

    def project(j, lo, hi):
        p_ref[j, :, lo:hi] = ctx[j]["scale"] * jnp.dot(
            h_ref[j], win_ref[:, lo:hi], preferred_element_type=F32)

    def stage_norm(j):
        x = x_ref[j]
        h_ref[j] = (x * nw_ref[...]).astype(BF16)
        ctx[j]["scale"] = _rms_scale(x)

    def stage_attention_columns(j):
        project(j, OFF_R, OFF_U)

    def stage_gate(j):
        r = p_ref[j, :, OFF_R:OFF_R + R_PAD].astype(BF16)
        ctx[j]["gate_pre"] = jnp.dot(r, wgk_ref[...], preferred_element_type=F32) + bgk_ref[...]

    def stage_gmlp_columns(j):
        project(j, OFF_U, OFF_G)

    def stage_cumulative_decay(j):
        gate_pre = ctx[j].pop("gate_pre")
        log_a = (jnp.minimum(gate_pre, 0.0) - jnp.log1p(jnp.exp(-jnp.abs(gate_pre)))) / GATE_NORM
        ctx[j]["cum3"] = [
            jnp.dot(tri, jnp.concatenate(_split3(log_a[c * CHUNK:(c + 1) * CHUNK, :]), axis=1),
                    preferred_element_type=F32) for c in range(n_chunks)]

    def stage_gate_columns(j):
        project(j, OFF_G, IN_COLS_PAD)

    def stage_updates(j):
        cum3 = ctx[j].pop("cum3")
        decays = []
        for c in range(n_chunks):
            t0 = c * CHUNK
            cum = cum3[c][:, :B_KEY] + cum3[c][:, B_KEY:2 * B_KEY] + cum3[c][:, 2 * B_KEY:]
            tot = cum[CHUNK - 1:CHUNK, :]
            decays.append(jnp.exp(tot))
            k_dec = (p_ref[j, t0:t0 + CHUNK, OFF_K:OFF_K + B_KEY]
                     * jnp.exp(tot - cum)).astype(BF16)
            v = p_ref[j, t0:t0 + CHUNK, OFF_VB:OFF_VB + B_WIDTH].astype(BF16)
            for pr in range(N_PAIRS):
                kl = slice(pr * PAIR_LANES, (pr + 1) * PAIR_LANES)
                vl = slice(pr * PAIR_ROWS, (pr + 1) * PAIR_ROWS)
                upd = lax.dot_general(v[:, vl], k_dec[:, kl], (((0,), (0,)), ((), ())),
                                      preferred_element_type=F32)
                upd_ref[j, c, pr] = jnp.where(same_head, upd, 0.0)
        ctx[j]["decays"] = decays

    def stage_gmlp(j):
        va = p_ref[j, :, OFF_VA:OFF_VA + A_WIDTH]
        vn = (va * _rms_scale(va) * nv_ref[...]).astype(BF16)
        for blk in range(rows // A_BLOCK):
            r0 = blk * A_BLOCK
            for hf in range(2):
                c0 = hf * half_width
                v_blk = vn[r0:r0 + A_BLOCK, c0:c0 + half_width]
                v_stack = jnp.concatenate(
                    [jnp.where(lane_group == g, v_blk, jnp.zeros_like(v_blk))
                     for g in range(half_groups)], axis=0)
                z = (jnp.dot(ws_half[hf], v_stack, preferred_element_type=F32)
                     + bs_ref[:, c0:c0 + half_width])
                u = p_ref[j, r0:r0 + A_BLOCK, OFF_U + c0:OFF_U + c0 + half_width]
                y_ref[j, r0:r0 + A_BLOCK, c0:c0 + half_width] = (u * z).astype(BF16)

    def stage_recurrence(j):
        decays = ctx[j].pop("decays")
        for pr in range(N_PAIRS):
            kl = slice(pr * PAIR_LANES, (pr + 1) * PAIR_LANES)
            st = st_ref[j, pr]
            for c in range(n_chunks):
                st = st * decays[c][:, kl] + upd_ref[j, c, pr]
                stb_ref[j, c, pr] = st.astype(BF16).T
            st_ref[j, pr] = st

    def stage_readout(j):
        for c in range(n_chunks):
            t0 = c * CHUNK
            for pr in range(N_PAIRS):
                q = (p_ref[j, t0:t0 + CHUNK,
                           OFF_Q + pr * PAIR_LANES:OFF_Q + (pr + 1) * PAIR_LANES]
                     * (B_HEAD_K ** -0.5)).astype(BF16)
                o = jnp.dot(q, stb_ref[j, c, pr], preferred_element_type=F32)
                p_ref[j, t0:t0 + CHUNK,
                      OFF_VB + pr * PAIR_ROWS:OFF_VB + (pr + 1) * PAIR_ROWS] = o

    def stage_out_a(j):
        ctx[j]["out_a"] = jnp.dot(y_ref[j, :, :A_WIDTH], wout_ref[:A_WIDTH, :],
                                  preferred_element_type=F32)

    def stage_attention_gate(j):
        for hd in range(B_HEADS):
            cols = slice(hd * B_HEAD_V, (hd + 1) * B_HEAD_V)
            o = p_ref[j, :, OFF_VB + hd * B_HEAD_V:OFF_VB + (hd + 1) * B_HEAD_V]
            o = o * _rms_scale(o) * no_ref[:, cols]
            g = p_ref[j, :, OFF_G + hd * B_HEAD_V:OFF_G + (hd + 1) * B_HEAD_V]
            y_ref[j, :, A_WIDTH + hd * B_HEAD_V:A_WIDTH + (hd + 1) * B_HEAD_V] = (
                o * (g * _sigmoid(g))).astype(BF16)

    def stage_out_b(j):
        out_b = jnp.dot(y_ref[j, :, A_WIDTH:], wout_ref[A_WIDTH:, :], preferred_element_type=F32)
        o_ref[j] = x_ref[j] + (ctx[j].pop("out_a") + out_b)

    stages = (stage_norm, stage_attention_columns, stage_gate, stage_gmlp_columns,
              stage_cumulative_decay, stage_gate_columns, stage_updates, stage_gmlp,
              stage_recurrence, stage_readout, stage_out_a, stage_attention_gate, stage_out_b)
    for stage in stages:
        for j in range(n_seqs):
            stage(j)


def mixer_call(x, norm_w, w_in_p, norm_v, w_s, b_s_full, w_gk_p, b_gk, norm_o, w_out):
    bsz, seq, _ = x.shape
    n_chunks = MIX_ROWS // CHUNK
    return pl.pallas_call(
        _mixer_kernel,
        out_shape=jax.ShapeDtypeStruct(x.shape, F32),
        grid=(bsz // MIX_SEQS, seq // MIX_ROWS),
        in_specs=[
            pl.BlockSpec((MIX_SEQS, MIX_ROWS, D_MODEL), lambda b, s: (b, s, 0)),
            _resident((1, D_MODEL)),
            _resident(w_in_p.shape),
            _resident((1, A_WIDTH)),
            _resident(w_s.shape),
            _resident(b_s_full.shape),
            _resident(w_gk_p.shape),
            _resident((1, B_KEY)),
            _resident((1, B_WIDTH)),
            _resident(w_out.shape),
        ],
        out_specs=pl.BlockSpec((MIX_SEQS, MIX_ROWS, D_MODEL), lambda b, s: (b, s, 0)),
        scratch_shapes=[
            pltpu.VMEM((MIX_SEQS, MIX_ROWS, D_MODEL), BF16),
            pltpu.VMEM((MIX_SEQS, MIX_ROWS, IN_COLS_PAD), F32),
            pltpu.VMEM((MIX_SEQS, MIX_ROWS, D_MODEL), BF16),
            pltpu.VMEM((MIX_SEQS, n_chunks, N_PAIRS, PAIR_ROWS, PAIR_LANES), F32),
            pltpu.VMEM((MIX_SEQS, n_chunks, N_PAIRS, PAIR_LANES, PAIR_ROWS), BF16),
            pltpu.VMEM((MIX_SEQS, N_PAIRS, PAIR_ROWS, PAIR_LANES), F32),
        ],
        compiler_params=pltpu.CompilerParams(
            dimension_semantics=("arbitrary", "arbitrary"), vmem_limit_bytes=VMEM_LIMIT),
        name="mixer",
    )(x, norm_w, w_in_p, norm_v, w_s, b_s_full, w_gk_p, b_gk, norm_o, w_out)


def _regroup_w_in(w):
    a2, bk, bw = 2 * A_WIDTH, B_KEY, B_WIDTH
    qkv = w[:, a2:a2 + 2 * bk + bw]
    g = w[:, a2 + 2 * bk + bw:a2 + 2 * bk + 2 * bw]
    r = jnp.pad(w[:, a2 + 2 * bk + 2 * bw:], ((0, 0), (0, R_PAD - GATE_RANK)))
    return jnp.concatenate([r, qkv, w[:, :a2], g], axis=1).astype(BF16)


def kernel(x, ff1_norm, ff1_w_in, ff1_w_out, mix_norm, w_in, gmlp_norm_v, gmlp_w_s, gmlp_b_s,
           gla_w_gk2, gla_b_gk, gla_norm_o, w_out, ff2_norm, ff2_w_in, ff2_w_out, final_norm):
    bsz, seq, d = x.shape
    depth = ff1_norm.shape[0]
    row = lambda v: v.reshape(1, -1)
    rows3 = lambda v: v.reshape(depth, 1, -1)
    fw = row(final_norm)
    ff1_nw, ff2_nw = rows3(ff1_norm), rows3(ff2_norm)
    for l in range(depth):
        w_in_p = _regroup_w_in(w_in[l])
        w_gk_p = jnp.pad(gla_w_gk2[l], ((0, R_PAD - GATE_RANK), (0, 0))).astype(BF16)
        b_s_full = jnp.repeat(gmlp_b_s[l].T, A_GROUP_DIM, axis=1)

        x = ffn_call(x.reshape(bsz * seq, d), ff1_nw, ff1_w_in, ff1_w_out, fw,
                     layer=l, final_norm=False).reshape(bsz, seq, d)
        x = mixer_call(x, row(mix_norm[l]), w_in_p, row(gmlp_norm_v[l]), gmlp_w_s[l], b_s_full,
                       w_gk_p, row(gla_b_gk[l]), row(gla_norm_o[l]), w_out[l].astype(BF16))
        x = ffn_call(x.reshape(bsz * seq, d), ff2_nw, ff2_w_in, ff2_w_out, fw,
                     layer=l, final_norm=(l == depth - 1)).reshape(bsz, seq, d)
    return x
```

```python
import functools

import jax
import jax.numpy as jnp
from jax import lax
from jax.experimental import pallas as pl
from jax.experimental.pallas import tpu as pltpu

D_MODEL = 1024
D_FF = 2816
CHUNK = 64
A_WIDTH = 512
A_GROUPS = 8
A_GROUP_DIM = A_WIDTH // A_GROUPS
A_BLOCK = 128
B_WIDTH = 512
B_HEADS = 4
B_HEAD_V = 128
B_HEAD_K = 64
B_KEY = B_HEADS * B_HEAD_K
GATE_RANK = 16
GATE_NORM = 16.0
EPS = 1e-6

LANES = 128
R_PAD = LANES
OFF_R = 0
OFF_Q = OFF_R + R_PAD
OFF_K = OFF_Q + B_KEY
OFF_VB = OFF_K + B_KEY
OFF_U = OFF_VB + B_WIDTH
OFF_VA = OFF_U + A_WIDTH
OFF_G = OFF_VA + A_WIDTH
IN_COLS_PAD = OFF_G + B_WIDTH

FF_CHUNK = 256
N_FF_CHUNKS = D_FF // FF_CHUNK
FFN_ROWS = 1024
MIX_ROWS = 512
MIX_SEQS = 2
VMEM_LIMIT = 56 * 1024 * 1024

PAIR_ROWS = 2 * B_HEAD_V
PAIR_LANES = 2 * B_HEAD_K
N_PAIRS = B_HEADS // 2

BF16 = jnp.bfloat16
F32 = jnp.float32


def _rms_scale(x):
    return lax.rsqrt(jnp.mean(x * x, axis=-1, keepdims=True) + EPS)


def _sigmoid(x):
    return 1.0 / (1.0 + jnp.exp(-x))


def _split3(x):
    hi = x.astype(BF16)
    r1 = x - hi.astype(F32)
    mid = r1.astype(BF16)
    lo = (r1 - mid.astype(F32)).astype(BF16)
    return hi, mid, lo


def _ffn_kernel(x_ref, nw_ref, win_ref, wout_ref, fw_ref, o_ref, xn_ref, *, final_norm):
    x = x_ref[...]
    xn_ref[...] = (x * nw_ref[...]).astype(BF16)
    scale = _rms_scale(x)

    def up(c):
        lo = c * FF_CHUNK
        xn = xn_ref[...]
        a = jnp.dot(xn, win_ref[:, lo:lo + FF_CHUNK].astype(BF16), preferred_element_type=F32)
        g = jnp.dot(xn, win_ref[:, D_FF + lo:D_FF + lo + FF_CHUNK].astype(BF16),
                    preferred_element_type=F32)
        return a, g

    nxt = up(0)
    for c in range(N_FF_CHUNKS):
        lo = c * FF_CHUNK
        a, g = nxt
        if c + 1 < N_FF_CHUNKS:
            nxt = up(c + 1)
        g = scale * g
        hh = (g * _sigmoid(g) * (scale * a)).astype(BF16)
        part = jnp.dot(hh, wout_ref[lo:lo + FF_CHUNK, :].astype(BF16),
                       preferred_element_type=F32)
        if c == 0:
            o_ref[...] = part
        elif c < N_FF_CHUNKS - 1:
            o_ref[...] += part
        else:
            y = x_ref[...] + 0.5 * (o_ref[...] + part)
            if final_norm:
                y = y * _rms_scale(y) * fw_ref[...]
            o_ref[...] = y


def _resident(shape):
    zeros = (0,) * len(shape)
    return pl.BlockSpec(shape, lambda *_: zeros, pipeline_mode=pl.Buffered(1))


def _layer_resident(shape, layer):
    index = (layer,) + (0,) * (len(shape) - 1)
    return pl.BlockSpec((None,) + tuple(shape[1:]), lambda *_: index,
                        pipeline_mode=pl.Buffered(1))


def ffn_call(x2d, norm_w, w_in, w_out, final_w, *, layer, final_norm):
    n = x2d.shape[0]
    return pl.pallas_call(
        functools.partial(_ffn_kernel, final_norm=final_norm),
        out_shape=jax.ShapeDtypeStruct(x2d.shape, F32),
        grid=(n // FFN_ROWS,),
        in_specs=[
            pl.BlockSpec((FFN_ROWS, D_MODEL), lambda i: (i, 0)),
            _layer_resident(norm_w.shape, layer),
            _layer_resident(w_in.shape, layer),
            _layer_resident(w_out.shape, layer),
            _resident((1, D_MODEL)),
        ],
        out_specs=pl.BlockSpec((FFN_ROWS, D_MODEL), lambda i: (i, 0)),
        scratch_shapes=[
            pltpu.VMEM((FFN_ROWS, D_MODEL), BF16),
        ],
        compiler_params=pltpu.CompilerParams(
            dimension_semantics=("arbitrary",), vmem_limit_bytes=VMEM_LIMIT),
        name="ffn",
    )(x2d, norm_w, w_in, w_out, final_w)


def _mixer_kernel(x_ref, nw_ref, win_ref, nv_ref, ws_ref, bs_ref, wgk_ref, bgk_ref, no_ref,
                  wout_ref, o_ref, h_ref, p_ref, y_ref, upd_ref, stb_ref, st_ref):
    n_seqs, rows = x_ref.shape[0], x_ref.shape[1]
    n_chunks = rows // CHUNK

    @pl.when(pl.program_id(1) == 0)
    def _():
        st_ref[...] = jnp.zeros_like(st_ref)

    tri = (lax.broadcasted_iota(jnp.int32, (CHUNK, CHUNK), 0)
           >= lax.broadcasted_iota(jnp.int32, (CHUNK, CHUNK), 1)).astype(BF16)
    same_head = (lax.broadcasted_iota(jnp.int32, (PAIR_ROWS, PAIR_LANES), 0) // B_HEAD_V
                 == lax.broadcasted_iota(jnp.int32, (PAIR_ROWS, PAIR_LANES), 1) // B_HEAD_K)
    t_chunk = lax.broadcasted_iota(jnp.int32, (A_BLOCK, A_BLOCK), 0) // CHUNK
    s_chunk = lax.broadcasted_iota(jnp.int32, (A_BLOCK, A_BLOCK), 1) // CHUNK
    causal = t_chunk >= s_chunk
    half_groups = A_GROUPS // 2
    half_width = A_WIDTH // 2
    ws_half = [
        jnp.concatenate([jnp.where(causal, ws_ref[hf * half_groups + g], 0.0).astype(BF16)
                         for g in range(half_groups)], axis=1)
        for hf in range(2)]
    lane_group = lax.broadcasted_iota(jnp.int32, (A_BLOCK, half_width), 1) // A_GROUP_DIM

    ctx = [dict() for _ in range(n_seqs)]

    def project(j, lo, hi):
        p_ref[j, :, lo:hi] = ctx[j]["scale"] * jnp.dot(
            h_ref[j], win_ref[:, lo:hi], preferred_element_type=F32)

    def stage_norm(j):
        x = x_ref[j]
        h_ref[j] = (x * nw_ref[...]).astype(BF16)
        ctx[j]["scale"] = _rms_scale(x)

    def stage_attention_columns(j):
        project(j, OFF_R, OFF_U)

    def stage_gate(j):
        r = p_ref[j, :, OFF_R:OFF_R + R_PAD].astype(BF16)
        ctx[j]["gate_pre"] = jnp.dot(r, wgk_ref[...], preferred_element_type=F32) + bgk_ref[...]

    def stage_gmlp_columns(j):
        project(j, OFF_U, OFF_G)

    def stage_cumulative_decay(j):
        gate_pre = ctx[j].pop("gate_pre")
        log_a = (jnp.minimum(gate_pre, 0.0) - jnp.log1p(jnp.exp(-jnp.abs(gate_pre)))) / GATE_NORM
        ctx[j]["cum3"] = [
            jnp.dot(tri, jnp.concatenate(_split3(log_a[c * CHUNK:(c + 1) * CHUNK, :]), axis=1),
                    preferred_element_type=F32) for c in range(n_chunks)]

    def stage_gate_columns(j):
        project(j, OFF_G, IN_COLS_PAD)

    def stage_updates(j):
        cum3 = ctx[j].pop("cum3")
        decays = []
        for c in range(n_chunks):
            t0 = c * CHUNK
            cum = cum3[c][:, :B_KEY] + cum3[c][:, B_KEY:2 * B_KEY] + cum3[c][:, 2 * B_KEY:]
            tot = cum[CHUNK - 1:CHUNK, :]
            decays.append(jnp.exp(tot))
            k_dec = (p_ref[j, t0:t0 + CHUNK, OFF_K:OFF_K + B_KEY]
                     * jnp.exp(tot - cum)).astype(BF16)
            v = p_ref[j, t0:t0 + CHUNK, OFF_VB:OFF_VB + B_WIDTH].astype(BF16)
            for pr in range(N_PAIRS):
                kl = slice(pr * PAIR_LANES, (pr + 1) * PAIR_LANES)
                vl = slice(pr * PAIR_ROWS, (pr + 1) * PAIR_ROWS)
                upd = lax.dot_general(v[:, vl], k_dec[:, kl], (((0,), (0,)), ((), ())),
                                      preferred_element_type=F32)
                upd_ref[j, c, pr] = jnp.where(same_head, upd, 0.0)
        ctx[j]["decays"] = decays

    def stage_gmlp(j):
        va = p_ref[j, :, OFF_VA:OFF_VA + A_WIDTH]
        vn = (va * _rms_scale(va) * nv_ref[...]).astype(BF16)
        for blk in range(rows // A_BLOCK):
            r0 = blk * A_BLOCK
            for hf in range(2):
                c0 = hf * half_width
                v_blk = vn[r0:r0 + A_BLOCK, c0:c0 + half_width]
                v_stack = jnp.concatenate(
                    [jnp.where(lane_group == g, v_blk, jnp.zeros_like(v_blk))
                     for g in range(half_groups)], axis=0)
                z = (jnp.dot(ws_half[hf], v_stack, preferred_element_type=F32)
                     + bs_ref[:, c0:c0 + half_width])
                u = p_ref[j, r0:r0 + A_BLOCK, OFF_U + c0:OFF_U + c0 + half_width]
                y_ref[j, r0:r0 + A_BLOCK, c0:c0 + half_width] = (u * z).astype(BF16)

    def stage_recurrence(j):
        decays = ctx[j].pop("decays")
        for pr in range(N_PAIRS):
            kl = slice(pr * PAIR_LANES, (pr + 1) * PAIR_LANES)
            st = st_ref[j, pr]
            for c in range(n_chunks):
                st = st * decays[c][:, kl] + upd_ref[j, c, pr]
                stb_ref[j, c, pr] = st.astype(BF16).T
            st_ref[j, pr] = st

    def stage_readout(j):
        for c in range(n_chunks):
            t0 = c * CHUNK
            for pr in range(N_PAIRS):
                q = (p_ref[j, t0:t0 + CHUNK,
                           OFF_Q + pr * PAIR_LANES:OFF_Q + (pr + 1) * PAIR_LANES]
                     * (B_HEAD_K ** -0.5)).astype(BF16)
                o = jnp.dot(q, stb_ref[j, c, pr], preferred_element_type=F32)
                p_ref[j, t0:t0 + CHUNK,
                      OFF_VB + pr * PAIR_ROWS:OFF_VB + (pr + 1) * PAIR_ROWS] = o

    def stage_out_a(j):
        ctx[j]["out_a"] = jnp.dot(y_ref[j, :, :A_WIDTH], wout_ref[:A_WIDTH, :],
                                  preferred_element_type=F32)

    def stage_attention_gate(j):
        for hd in range(B_HEADS):
            cols = slice(hd * B_HEAD_V, (hd + 1) * B_HEAD_V)
            o = p_ref[j, :, OFF_VB + hd * B_HEAD_V:OFF_VB + (hd + 1) * B_HEAD_V]
            o = o * _rms_scale(o) * no_ref[:, cols]
            g = p_ref[j, :, OFF_G + hd * B_HEAD_V:OFF_G + (hd + 1) * B_HEAD_V]
            y_ref[j, :, A_WIDTH + hd * B_HEAD_V:A_WIDTH + (hd + 1) * B_HEAD_V] = (
                o * (g * _sigmoid(g))).astype(BF16)

    def stage_out_b(j):
        out_b = jnp.dot(y_ref[j, :, A_WIDTH:], wout_ref[A_WIDTH:, :], preferred_element_type=F32)
        o_ref[j] = x_ref[j] + (ctx[j].pop("out_a") + out_b)

    stages = (stage_norm, stage_attention_columns, stage_gate, stage_gmlp_columns,
              stage_cumulative_decay, stage_gate_columns, stage_updates, stage_gmlp,
              stage_recurrence, stage_readout, stage_out_a, stage_attention_gate, stage_out_b)
    for stage in stages:
        for j in range(n_seqs):
            stage(j)


def mixer_call(x, norm_w, w_in_p, norm_v, w_s, b_s_full, w_gk_p, b_gk, norm_o, w_out):
    bsz, seq, _ = x.shape
    n_chunks = MIX_ROWS // CHUNK
    return pl.pallas_call(
        _mixer_kernel,
        out_shape=jax.ShapeDtypeStruct(x.shape, F32),
        grid=(bsz // MIX_SEQS, seq // MIX_ROWS),
        in_specs=[
            pl.BlockSpec((MIX_SEQS, MIX_ROWS, D_MODEL), lambda b, s: (b, s, 0)),
            _resident((1, D_MODEL)),
            _resident(w_in_p.shape),
            _resident((1, A_WIDTH)),
            _resident(w_s.shape),
            _resident(b_s_full.shape),
            _resident(w_gk_p.shape),
            _resident((1, B_KEY)),
            _resident((1, B_WIDTH)),
            _resident(w_out.shape),
        ],
        out_specs=pl.BlockSpec((MIX_SEQS, MIX_ROWS, D_MODEL), lambda b, s: (b, s, 0)),
        scratch_shapes=[
            pltpu.VMEM((MIX_SEQS, MIX_ROWS, D_MODEL), BF16),
            pltpu.VMEM((MIX_SEQS, MIX_ROWS, IN_COLS_PAD), F32),
            pltpu.VMEM((MIX_SEQS, MIX_ROWS, D_MODEL), BF16),
            pltpu.VMEM((MIX_SEQS, n_chunks, N_PAIRS, PAIR_ROWS, PAIR_LANES), F32),
            pltpu.VMEM((MIX_SEQS, n_chunks, N_PAIRS, PAIR_LANES, PAIR_ROWS), BF16),
            pltpu.VMEM((MIX_SEQS, N_PAIRS, PAIR_ROWS, PAIR_LANES), F32),
        ],
        compiler_params=pltpu.CompilerParams(
            dimension_semantics=("arbitrary", "arbitrary"), vmem_limit_bytes=VMEM_LIMIT),
        name="mixer",
    )(x, norm_w, w_in_p, norm_v, w_s, b_s_full, w_gk_p, b_gk, norm_o, w_out)


def _regroup_w_in(w):
    a2, bk, bw = 2 * A_WIDTH, B_KEY, B_WIDTH
    qkv = w[:, a2:a2 + 2 * bk + bw]
    g = w[:, a2 + 2 * bk + bw:a2 + 2 * bk + 2 * bw]
    r = jnp.pad(w[:, a2 + 2 * bk + 2 * bw:], ((0, 0), (0, R_PAD - GATE_RANK)))
    return jnp.concatenate([r, qkv, w[:, :a2], g], axis=1).astype(BF16)


def kernel(x, ff1_norm, ff1_w_in, ff1_w_out, mix_norm, w_in, gmlp_norm_v, gmlp_w_s, gmlp_b_s,
           gla_w_gk2, gla_b_gk, gla_norm_o, w_out, ff2_norm, ff2_w_in, ff2_w_out, final_norm):
    bsz, seq, d = x.shape
    depth = ff1_norm.shape[0]
    row = lambda v: v.reshape(1, -1)
    rows3 = lambda v: v.reshape(depth, 1, -1)
    fw = row(final_norm)
    ff1_nw, ff2_nw = rows3(ff1_norm), rows3(ff2_norm)
    for l in range(depth):
        w_in_p = _regroup_w_in(w_in[l])
        w_gk_p = jnp.pad(gla_w_gk2[l], ((0, R_PAD - GATE_RANK), (0, 0))).astype(BF16)
        b_s_full = jnp.repeat(gmlp_b_s[l].T, A_GROUP_DIM, axis=1)

        x = ffn_call(x.reshape(bsz * seq, d), ff1_nw, ff1_w_in, ff1_w_out, fw,
                     layer=l, final_norm=False).reshape(bsz, seq, d)
        x = mixer_call(x, row(mix_norm[l]), w_in_p, row(gmlp_norm_v[l]), gmlp_w_s[l], b_s_full,
                       w_gk_p, row(gla_b_gk[l]), row(gla_norm_o[l]), w_out[l].astype(BF16))
        x = ffn_call(x.reshape(bsz * seq, d), ff2_nw, ff2_w_in, ff2_w_out, fw,
                     layer=l, final_norm=(l == depth - 1)).reshape(bsz, seq, d)
    return x
```

```python
import functools

import jax
import jax.numpy as jnp
from jax import lax
from jax.experimental import pallas as pl
from jax.experimental.pallas import tpu as pltpu

D_MODEL = 1024
D_FF = 2816
CHUNK = 64
A_WIDTH = 512
A_GROUPS = 8
A_GROUP_DIM = A_WIDTH // A_GROUPS
A_BLOCK = 128
B_WIDTH = 512
B_HEADS = 4
B_HEAD_V = 128
B_HEAD_K = 64
B_KEY = B_HEADS * B_HEAD_K
GATE_RANK = 16
GATE_NORM = 16.0
EPS = 1e-6

LANES = 128
R_PAD = LANES
OFF_R = 0
OFF_Q = OFF_R + R_PAD
OFF_K = OFF_Q + B_KEY
OFF_VB = OFF_K + B_KEY
OFF_U = OFF_VB + B_WIDTH
OFF_VA = OFF_U + A_WIDTH
OFF_G = OFF_VA + A_WIDTH
IN_COLS_PAD = OFF_G + B_WIDTH

FF_CHUNK = 256
N_FF_CHUNKS = D_FF // FF_CHUNK
FFN_ROWS = 1024
FFN_SUB = 512
MIX_ROWS = 512
MIX_SEQS = 2
VMEM_LIMIT = 56 * 1024 * 1024

PAIR_ROWS = 2 * B_HEAD_V
PAIR_LANES = 2 * B_HEAD_K
N_PAIRS = B_HEADS // 2

BF16 = jnp.bfloat16
F32 = jnp.float32


def _rms_scale(x):
    return lax.rsqrt(jnp.mean(x * x, axis=-1, keepdims=True) + EPS)


def _sigmoid(x):
    return 1.0 / (1.0 + jnp.exp(-x))


def _split3(x):
    hi = x.astype(BF16)
    r1 = x - hi.astype(F32)
    mid = r1.astype(BF16)
    lo = (r1 - mid.astype(F32)).astype(BF16)
    return hi, mid, lo


def _ffn_kernel(x_ref, nw_ref, win_ref, wout_ref, fw_ref, o_ref, xn_ref, acc_ref, *, final_norm):
    subs = [slice(s * FFN_SUB, (s + 1) * FFN_SUB) for s in range(FFN_ROWS // FFN_SUB)]

    scales = []
    for rows in subs:
        x = x_ref[rows, :]
        xn_ref[rows, :] = (x * nw_ref[...]).astype(BF16)
        scales.append(_rms_scale(x))

    def up(item):
        s, c = item
        lo = c * FF_CHUNK
        xn = xn_ref[subs[s], :]
        a = jnp.dot(xn, win_ref[:, lo:lo + FF_CHUNK].astype(BF16), preferred_element_type=F32)
        g = jnp.dot(xn, win_ref[:, D_FF + lo:D_FF + lo + FF_CHUNK].astype(BF16),
                    preferred_element_type=F32)
        return a, g

    items = [(s, c) for s in range(len(subs)) for c in range(N_FF_CHUNKS)]
    nxt = up(items[0])
    for k, (s, c) in enumerate(items):
        lo = c * FF_CHUNK
        rows = subs[s]
        a, g = nxt
        if k + 1 < len(items):
            nxt = up(items[k + 1])
        g = scales[s] * g
        hh = (g * _sigmoid(g) * (scales[s] * a)).astype(BF16)
        part = jnp.dot(hh, wout_ref[lo:lo + FF_CHUNK, :].astype(BF16),
                       preferred_element_type=F32)
        if c == 0:
            acc_ref[...] = part
        elif c < N_FF_CHUNKS - 1:
            acc_ref[...] += part
        else:
            y = x_ref[rows, :] + 0.5 * (acc_ref[...] + part)
            if final_norm:
                y = y * _rms_scale(y) * fw_ref[...]
            o_ref[rows, :] = y


def _resident(shape):
    zeros = (0,) * len(shape)
    return pl.BlockSpec(shape, lambda *_: zeros, pipeline_mode=pl.Buffered(1))


def _layer_resident(shape, layer):
    index = (layer,) + (0,) * (len(shape) - 1)
    return pl.BlockSpec((None,) + tuple(shape[1:]), lambda *_: index,
                        pipeline_mode=pl.Buffered(1))


def ffn_call(x2d, norm_w, w_in, w_out, final_w, *, layer, final_norm):
    n = x2d.shape[0]
    return pl.pallas_call(
        functools.partial(_ffn_kernel, final_norm=final_norm),
        out_shape=jax.ShapeDtypeStruct(x2d.shape, F32),
        grid=(n // FFN_ROWS,),
        in_specs=[
            pl.BlockSpec((FFN_ROWS, D_MODEL), lambda i: (i, 0)),
            _layer_resident(norm_w.shape, layer),
            _layer_resident(w_in.shape, layer),
            _layer_resident(w_out.shape, layer),
            _resident((1, D_MODEL)),
        ],
        out_specs=pl.BlockSpec((FFN_ROWS, D_MODEL), lambda i: (i, 0)),
        scratch_shapes=[
            pltpu.VMEM((FFN_ROWS, D_MODEL), BF16),
            pltpu.VMEM((FFN_SUB, D_MODEL), F32),
        ],
        compiler_params=pltpu.CompilerParams(
            dimension_semantics=("arbitrary",), vmem_limit_bytes=VMEM_LIMIT),
        name="ffn",
    )(x2d, norm_w, w_in, w_out, final_w)


def _mixer_kernel(x_ref, nw_ref, win_ref, nv_ref, ws_ref, bs_ref, wgk_ref, bgk_ref, no_ref,
                  wout_ref, o_ref, h_ref, p_ref, y_ref, upd_ref, stb_ref, st_ref):
    n_seqs, rows = x_ref.shape[0], x_ref.shape[1]
    n_chunks = rows // CHUNK

    @pl.when(pl.program_id(1) == 0)
    def _():
        st_ref[...] = jnp.zeros_like(st_ref)

    tri = (lax.broadcasted_iota(jnp.int32, (CHUNK, CHUNK), 0)
           >= lax.broadcasted_iota(jnp.int32, (CHUNK, CHUNK), 1)).astype(BF16)
    same_head = (lax.broadcasted_iota(jnp.int32, (PAIR_ROWS, PAIR_LANES), 0) // B_HEAD_V
                 == lax.broadcasted_iota(jnp.int32, (PAIR_ROWS, PAIR_LANES), 1) // B_HEAD_K)
    t_chunk = lax.broadcasted_iota(jnp.int32, (A_BLOCK, A_BLOCK), 0) // CHUNK
    s_chunk = lax.broadcasted_iota(jnp.int32, (A_BLOCK, A_BLOCK), 1) // CHUNK
    causal = t_chunk >= s_chunk
    half_groups = A_GROUPS // 2
    half_width = A_WIDTH // 2
    ws_half = [
        jnp.concatenate([jnp.where(causal, ws_ref[hf * half_groups + g], 0.0).astype(BF16)
                         for g in range(half_groups)], axis=1)
        for hf in range(2)]
    lane_group = lax.broadcasted_iota(jnp.int32, (A_BLOCK, half_width), 1) // A_GROUP_DIM

    ctx = [dict() for _ in range(n_seqs)]

    def project(j, lo, hi):
        p_ref[j, :, lo:hi] = ctx[j]["scale"] * jnp.dot(
            h_ref[j], win_ref[:, lo:hi], preferred_element_type=F32)

    def stage_norm(j):
        x = x_ref[j]
        h_ref[j] = (x * nw_ref[...]).astype(BF16)
        ctx[j]["scale"] = _rms_scale(x)

    def stage_attention_columns(j):
        project(j, OFF_R, OFF_U)

    def stage_gate(j):
        r = p_ref[j, :, OFF_R:OFF_R + R_PAD].astype(BF16)
        ctx[j]["gate_pre"] = jnp.dot(r, wgk_ref[...], preferred_element_type=F32) + bgk_ref[...]

    def stage_gmlp_columns(j):
        project(j, OFF_U, OFF_G)

    def stage_cumulative_decay(j):
        gate_pre = ctx[j].pop("gate_pre")
        log_a = (jnp.minimum(gate_pre, 0.0) - jnp.log1p(jnp.exp(-jnp.abs(gate_pre)))) / GATE_NORM
        ctx[j]["cum3"] = [
            jnp.dot(tri, jnp.concatenate(_split3(log_a[c * CHUNK:(c + 1) * CHUNK, :]), axis=1),
                    preferred_element_type=F32) for c in range(n_chunks)]

    def stage_gate_columns(j):
        project(j, OFF_G, IN_COLS_PAD)

    def stage_updates(j):
        cum3 = ctx[j].pop("cum3")
        decays = []
        for c in range(n_chunks):
            t0 = c * CHUNK
            cum = cum3[c][:, :B_KEY] + cum3[c][:, B_KEY:2 * B_KEY] + cum3[c][:, 2 * B_KEY:]
            tot = cum[CHUNK - 1:CHUNK, :]
            decays.append(jnp.exp(tot))
            k_dec = (p_ref[j, t0:t0 + CHUNK, OFF_K:OFF_K + B_KEY]
                     * jnp.exp(tot - cum)).astype(BF16)
            v = p_ref[j, t0:t0 + CHUNK, OFF_VB:OFF_VB + B_WIDTH].astype(BF16)
            for pr in range(N_PAIRS):
                kl = slice(pr * PAIR_LANES, (pr + 1) * PAIR_LANES)
                vl = slice(pr * PAIR_ROWS, (pr + 1) * PAIR_ROWS)
                upd = lax.dot_general(v[:, vl], k_dec[:, kl], (((0,), (0,)), ((), ())),
                                      preferred_element_type=F32)
                upd_ref[j, c, pr] = jnp.where(same_head, upd, 0.0)
        ctx[j]["decays"] = decays

    def stage_gmlp(j):
        va = p_ref[j, :, OFF_VA:OFF_VA + A_WIDTH]
        vn = (va * _rms_scale(va) * nv_ref[...]).astype(BF16)
        for blk in range(rows // A_BLOCK):
            r0 = blk * A_BLOCK
            for hf in range(2):
                c0 = hf * half_width
                v_blk = vn[r0:r0 + A_BLOCK, c0:c0 + half_width]
                v_stack = jnp.concatenate(
                    [jnp.where(lane_group == g, v_blk, jnp.zeros_like(v_blk))
                     for g in range(half_groups)], axis=0)
                z = (jnp.dot(ws_half[hf], v_stack, preferred_element_type=F32)
                     + bs_ref[:, c0:c0 + half_width])
                u = p_ref[j, r0:r0 + A_BLOCK, OFF_U + c0:OFF_U + c0 + half_width]
                y_ref[j, r0:r0 + A_BLOCK, c0:c0 + half_width] = (u * z).astype(BF16)

    def stage_recurrence(j):
        decays = ctx[j].pop("decays")
        for pr in range(N_PAIRS):
            kl = slice(pr * PAIR_LANES, (pr + 1) * PAIR_LANES)
            st = st_ref[j, pr]
            for c in range(n_chunks):
                st = st * decays[c][:, kl] + upd_ref[j, c, pr]
                stb_ref[j, c, pr] = st.astype(BF16).T
            st_ref[j, pr] = st

    def stage_readout(j):
        for c in range(n_chunks):
            t0 = c * CHUNK
            for pr in range(N_PAIRS):
                q = (p_ref[j, t0:t0 + CHUNK,
                           OFF_Q + pr * PAIR_LANES:OFF_Q + (pr + 1) * PAIR_LANES]
                     * (B_HEAD_K ** -0.5)).astype(BF16)
                o = jnp.dot(q, stb_ref[j, c, pr], preferred_element_type=F32)
                p_ref[j, t0:t0 + CHUNK,
                      OFF_VB + pr * PAIR_ROWS:OFF_VB + (pr + 1) * PAIR_ROWS] = o

    def stage_out_a(j):
        ctx[j]["out_a"] = jnp.dot(y_ref[j, :, :A_WIDTH], wout_ref[:A_WIDTH, :],
                                  preferred_element_type=F32)

    def stage_attention_gate(j):
        for hd in range(B_HEADS):
            cols = slice(hd * B_HEAD_V, (hd + 1) * B_HEAD_V)
            o = p_ref[j, :, OFF_VB + hd * B_HEAD_V:OFF_VB + (hd + 1) * B_HEAD_V]
            o = o * _rms_scale(o) * no_ref[:, cols]
            g = p_ref[j, :, OFF_G + hd * B_HEAD_V:OFF_G + (hd + 1) * B_HEAD_V]
            y_ref[j, :, A_WIDTH + hd * B_HEAD_V:A_WIDTH + (hd + 1) * B_HEAD_V] = (
                o * (g * _sigmoid(g))).astype(BF16)

    def stage_out_b(j):
        out_b = jnp.dot(y_ref[j, :, A_WIDTH:], wout_ref[A_WIDTH:, :], preferred_element_type=F32)
        o_ref[j] = x_ref[j] + (ctx[j].pop("out_a") + out_b)

    stages = (stage_norm, stage_attention_columns, stage_gate, stage_gmlp_columns,
              stage_cumulative_decay, stage_gate_columns, stage_updates, stage_gmlp,
              stage_recurrence, stage_readout, stage_out_a, stage_attention_gate, stage_out_b)
    for stage in stages:
        for j in range(n_seqs):
            stage(j)


def mixer_call(x, norm_w, w_in_p, norm_v, w_s, b_s_full, w_gk_p, b_gk, norm_o, w_out):
    bsz, seq, _ = x.shape
    n_chunks = MIX_ROWS // CHUNK
    return pl.pallas_call(
        _mixer_kernel,
        out_shape=jax.ShapeDtypeStruct(x.shape, F32),
        grid=(bsz // MIX_SEQS, seq // MIX_ROWS),
        in_specs=[
            pl.BlockSpec((MIX_SEQS, MIX_ROWS, D_MODEL), lambda b, s: (b, s, 0)),
            _resident((1, D_MODEL)),
            _resident(w_in_p.shape),
            _resident((1, A_WIDTH)),
            _resident(w_s.shape),
            _resident(b_s_full.shape),
            _resident(w_gk_p.shape),
            _resident((1, B_KEY)),
            _resident((1, B_WIDTH)),
            _resident(w_out.shape),
        ],
        out_specs=pl.BlockSpec((MIX_SEQS, MIX_ROWS, D_MODEL), lambda b, s: (b, s, 0)),
        scratch_shapes=[
            pltpu.VMEM((MIX_SEQS, MIX_ROWS, D_MODEL), BF16),
            pltpu.VMEM((MIX_SEQS, MIX_ROWS, IN_COLS_PAD), F32),
            pltpu.VMEM((MIX_SEQS, MIX_ROWS, D_MODEL), BF16),
            pltpu.VMEM((MIX_SEQS, n_chunks, N_PAIRS, PAIR_ROWS, PAIR_LANES), F32),
            pltpu.VMEM((MIX_SEQS, n_chunks, N_PAIRS, PAIR_LANES, PAIR_ROWS), BF16),
            pltpu.VMEM((MIX_SEQS, N_PAIRS, PAIR_ROWS, PAIR_LANES), F32),
        ],
        compiler_params=pltpu.CompilerParams(
            dimension_semantics=("arbitrary", "arbitrary"), vmem_limit_bytes=VMEM_LIMIT),
        name="mixer",
    )(x, norm_w, w_in_p, norm_v, w_s, b_s_full, w_gk_p, b_gk, norm_o, w_out)


def _regroup_w_in(w):
    a2, bk, bw = 2 * A_WIDTH, B_KEY, B_WIDTH
    qkv = w[:, a2:a2 + 2 * bk + bw]
    g = w[:, a2 + 2 * bk + bw:a2 + 2 * bk + 2 * bw]
    r = jnp.pad(w[:, a2 + 2 * bk + 2 * bw:], ((0, 0), (0, R_PAD - GATE_RANK)))
    return jnp.concatenate([r, qkv, w[:, :a2], g], axis=1).astype(BF16)


def kernel(x, ff1_norm, ff1_w_in, ff1_w_out, mix_norm, w_in, gmlp_norm_v, gmlp_w_s, gmlp_b_s,
           gla_w_gk2, gla_b_gk, gla_norm_o, w_out, ff2_norm, ff2_w_in, ff2_w_out, final_norm):
    bsz, seq, d = x.shape
    depth = ff1_norm.shape[0]
    row = lambda v: v.reshape(1, -1)
    rows3 = lambda v: v.reshape(depth, 1, -1)
    fw = row(final_norm)
    ff1_nw, ff2_nw = rows3(ff1_norm), rows3(ff2_norm)
    for l in range(depth):
        w_in_p = _regroup_w_in(w_in[l])
        w_gk_p = jnp.pad(gla_w_gk2[l], ((0, R_PAD - GATE_RANK), (0, 0))).astype(BF16)
        b_s_full = jnp.repeat(gmlp_b_s[l].T, A_GROUP_DIM, axis=1)

        x = ffn_call(x.reshape(bsz * seq, d), ff1_nw, ff1_w_in, ff1_w_out, fw,
                     layer=l, final_norm=False).reshape(bsz, seq, d)
        x = mixer_call(x, row(mix_norm[l]), w_in_p, row(gmlp_norm_v[l]), gmlp_w_s[l], b_s_full,
                       w_gk_p, row(gla_b_gk[l]), row(gla_norm_o[l]), w_out[l].astype(BF16))
        x = ffn_call(x.reshape(bsz * seq, d), ff2_nw, ff2_w_in, ff2_w_out, fw,
                     layer=l, final_norm=(l == depth - 1)).reshape(bsz, seq, d)
    return x
```

```python
import functools

import jax
import jax.numpy as jnp
from jax import lax
from jax.experimental import pallas as pl
from jax.experimental.pallas import tpu as pltpu

D_MODEL = 1024
D_FF = 2816
CHUNK = 64
A_WIDTH = 512
A_GROUPS = 8
A_GROUP_DIM = A_WIDTH // A_GROUPS
A_BLOCK = 128
B_WIDTH = 512
B_HEADS = 4
B_HEAD_V = 128
B_HEAD_K = 64
B_KEY = B_HEADS * B_HEAD_K
GATE_RANK = 16
GATE_NORM = 16.0
EPS = 1e-6

LANES = 128
R_PAD = LANES
OFF_R = 0
OFF_Q = OFF_R + R_PAD
OFF_K = OFF_Q + B_KEY
OFF_VB = OFF_K + B_KEY
OFF_U = OFF_VB + B_WIDTH
OFF_VA = OFF_U + A_WIDTH
OFF_G = OFF_VA + A_WIDTH
IN_COLS_PAD = OFF_G + B_WIDTH

FF_CHUNK = 256
N_FF_CHUNKS = D_FF // FF_CHUNK
N_STAGES = 8
STAGE_IN_ROWS = D_MODEL // N_STAGES
STAGE_OUT_ROWS = D_FF // N_STAGES
FFN_ROWS = 1024
FFN_SUB = 512
MIX_ROWS = 512
MIX_SEQS = 2
VMEM_LIMIT = 56 * 1024 * 1024

PAIR_ROWS = 2 * B_HEAD_V
PAIR_LANES = 2 * B_HEAD_K
N_PAIRS = B_HEADS // 2

BF16 = jnp.bfloat16
F32 = jnp.float32


def _rms_scale(x):
    return lax.rsqrt(jnp.mean(x * x, axis=-1, keepdims=True) + EPS)


def _sigmoid(x):
    return 1.0 / (1.0 + jnp.exp(-x))


def _split3(x):
    hi = x.astype(BF16)
    r1 = x - hi.astype(F32)
    mid = r1.astype(BF16)
    lo = (r1 - mid.astype(F32)).astype(BF16)
    return hi, mid, lo


def _ffn_kernel(x_ref, nw_ref, win_hbm, wout_hbm, fw_ref, o_ref, xn_ref, acc_ref, win_ref,
                wout_ref, stage_in, stage_out, sem, *, layer, final_norm):
    def stage_copies(k, slot):
        rows_in = pl.ds(k * STAGE_IN_ROWS, STAGE_IN_ROWS)
        rows_out = pl.ds(k * STAGE_OUT_ROWS, STAGE_OUT_ROWS)
        return (pltpu.make_async_copy(win_hbm.at[layer, rows_in, :], stage_in.at[slot],
                                      sem.at[slot, 0]),
                pltpu.make_async_copy(wout_hbm.at[layer, rows_out, :], stage_out.at[slot],
                                      sem.at[slot, 1]))

    @pl.when(pl.program_id(0) == 0)
    def _():
        for cp in stage_copies(0, 0):
            cp.start()
        for k in range(N_STAGES):
            slot = k % 2
            if k + 1 < N_STAGES:
                for cp in stage_copies(k + 1, 1 - slot):
                    cp.start()
            for cp in stage_copies(k, slot):
                cp.wait()
            win_ref[k * STAGE_IN_ROWS:(k + 1) * STAGE_IN_ROWS, :] = stage_in[slot].astype(BF16)
            wout_ref[k * STAGE_OUT_ROWS:(k + 1) * STAGE_OUT_ROWS, :] = (
                stage_out[slot].astype(BF16))

    subs = [slice(s * FFN_SUB, (s + 1) * FFN_SUB) for s in range(FFN_ROWS // FFN_SUB)]

    scales = []
    for rows in subs:
        x = x_ref[rows, :]
        xn_ref[rows, :] = (x * nw_ref[...]).astype(BF16)
        scales.append(_rms_scale(x))

    def up(item):
        s, c = item
        lo = c * FF_CHUNK
        xn = xn_ref[subs[s], :]
        a = jnp.dot(xn, win_ref[:, lo:lo + FF_CHUNK], preferred_element_type=F32)
        g = jnp.dot(xn, win_ref[:, D_FF + lo:D_FF + lo + FF_CHUNK], preferred_element_type=F32)
        return a, g

    items = [(s, c) for s in range(len(subs)) for c in range(N_FF_CHUNKS)]
    nxt = up(items[0])
    for k, (s, c) in enumerate(items):
        lo = c * FF_CHUNK
        rows = subs[s]
        a, g = nxt
        if k + 1 < len(items):
            nxt = up(items[k + 1])
        g = scales[s] * g
        hh = (g * _sigmoid(g) * (scales[s] * a)).astype(BF16)
        part = jnp.dot(hh, wout_ref[lo:lo + FF_CHUNK, :], preferred_element_type=F32)
        if c == 0:
            acc_ref[...] = part
        elif c < N_FF_CHUNKS - 1:
            acc_ref[...] += part
        else:
            y = x_ref[rows, :] + 0.5 * (acc_ref[...] + part)
            if final_norm:
                y = y * _rms_scale(y) * fw_ref[...]
            o_ref[rows, :] = y


def _resident(shape):
    zeros = (0,) * len(shape)
    return pl.BlockSpec(shape, lambda *_: zeros, pipeline_mode=pl.Buffered(1))


def _layer_resident(shape, layer):
    index = (layer,) + (0,) * (len(shape) - 1)
    return pl.BlockSpec((None,) + tuple(shape[1:]), lambda *_: index,
                        pipeline_mode=pl.Buffered(1))


def ffn_call(x2d, norm_w, w_in, w_out, final_w, *, layer, final_norm):
    n = x2d.shape[0]
    return pl.pallas_call(
        functools.partial(_ffn_kernel, layer=layer, final_norm=final_norm),
        out_shape=jax.ShapeDtypeStruct(x2d.shape, F32),
        grid=(n // FFN_ROWS,),
        in_specs=[
            pl.BlockSpec((FFN_ROWS, D_MODEL), lambda i: (i, 0)),
            _layer_resident(norm_w.shape, layer),
            pl.BlockSpec(memory_space=pl.ANY),
            pl.BlockSpec(memory_space=pl.ANY),
            _resident((1, D_MODEL)),
        ],
        out_specs=pl.BlockSpec((FFN_ROWS, D_MODEL), lambda i: (i, 0)),
        scratch_shapes=[
            pltpu.VMEM((FFN_ROWS, D_MODEL), BF16),
            pltpu.VMEM((FFN_SUB, D_MODEL), F32),
            pltpu.VMEM((D_MODEL, 2 * D_FF), BF16),
            pltpu.VMEM((D_FF, D_MODEL), BF16),
            pltpu.VMEM((2, STAGE_IN_ROWS, 2 * D_FF), F32),
            pltpu.VMEM((2, STAGE_OUT_ROWS, D_MODEL), F32),
            pltpu.SemaphoreType.DMA((2, 2)),
        ],
        compiler_params=pltpu.CompilerParams(
            dimension_semantics=("arbitrary",), vmem_limit_bytes=VMEM_LIMIT),
        name="ffn",
    )(x2d, norm_w, w_in, w_out, final_w)


def _mixer_kernel(x_ref, nw_ref, win_ref, nv_ref, ws_ref, bs_ref, wgk_ref, bgk_ref, no_ref,
                  wout_ref, o_ref, h_ref, p_ref, y_ref, upd_ref, stb_ref, st_ref):
    n_seqs, rows = x_ref.shape[0], x_ref.shape[1]
    n_chunks = rows // CHUNK

    @pl.when(pl.program_id(1) == 0)
    def _():
        st_ref[...] = jnp.zeros_like(st_ref)

    tri = (lax.broadcasted_iota(jnp.int32, (CHUNK, CHUNK), 0)
           >= lax.broadcasted_iota(jnp.int32, (CHUNK, CHUNK), 1)).astype(BF16)
    same_head = (lax.broadcasted_iota(jnp.int32, (PAIR_ROWS, PAIR_LANES), 0) // B_HEAD_V
                 == lax.broadcasted_iota(jnp.int32, (PAIR_ROWS, PAIR_LANES), 1) // B_HEAD_K)
    t_chunk = lax.broadcasted_iota(jnp.int32, (A_BLOCK, A_BLOCK), 0) // CHUNK
    s_chunk = lax.broadcasted_iota(jnp.int32, (A_BLOCK, A_BLOCK), 1) // CHUNK
    causal = t_chunk >= s_chunk
    half_groups = A_GROUPS // 2
    half_width = A_WIDTH // 2
    ws_half = [
        jnp.concatenate([jnp.where(causal, ws_ref[hf * half_groups + g], 0.0).astype(BF16)
                         for g in range(half_groups)], axis=1)
        for hf in range(2)]
    lane_group = lax.broadcasted_iota(jnp.int32, (A_BLOCK, half_width), 1) // A_GROUP_DIM

    ctx = [dict() for _ in range(n_seqs)]

    def project(j, lo, hi):
        p_ref[j, :, lo:hi] = ctx[j]["scale"] * jnp.dot(
            h_ref[j], win_ref[:, lo:hi], preferred_element_type=F32)

    def stage_norm(j):
        x = x_ref[j]
        h_ref[j] = (x * nw_ref[...]).astype(BF16)
        ctx[j]["scale"] = _rms_scale(x)

    def stage_rank_columns(j):
        project(j, OFF_R, OFF_Q)

    def stage_attention_columns(j):
        project(j, OFF_Q, OFF_U)

    def stage_gate(j):
        r = p_ref[j, :, OFF_R:OFF_R + R_PAD].astype(BF16)
        ctx[j]["gate_pre"] = jnp.dot(r, wgk_ref[...], preferred_element_type=F32) + bgk_ref[...]

    def stage_gmlp_columns(j):
        project(j, OFF_U, OFF_G)

    def stage_cumulative_decay(j):
        gate_pre = ctx[j].pop("gate_pre")
        log_a = (jnp.minimum(gate_pre, 0.0) - jnp.log1p(jnp.exp(-jnp.abs(gate_pre)))) / GATE_NORM
        ctx[j]["cum3"] = [
            jnp.dot(tri, jnp.concatenate(_split3(log_a[c * CHUNK:(c + 1) * CHUNK, :]), axis=1),
                    preferred_element_type=F32) for c in range(n_chunks)]

    def stage_gate_columns(j):
        project(j, OFF_G, IN_COLS_PAD)

    def stage_updates(j):
        cum3 = ctx[j].pop("cum3")
        decays = []
        for c in range(n_chunks):
            t0 = c * CHUNK
            cum = cum3[c][:, :B_KEY] + cum3[c][:, B_KEY:2 * B_KEY] + cum3[c][:, 2 * B_KEY:]
            tot = cum[CHUNK - 1:CHUNK, :]
            decays.append(jnp.exp(tot))
            k_dec = (p_ref[j, t0:t0 + CHUNK, OFF_K:OFF_K + B_KEY]
                     * jnp.exp(tot - cum)).astype(BF16)
            v = p_ref[j, t0:t0 + CHUNK, OFF_VB:OFF_VB + B_WIDTH].astype(BF16)
            for pr in range(N_PAIRS):
                kl = slice(pr * PAIR_LANES, (pr + 1) * PAIR_LANES)
                vl = slice(pr * PAIR_ROWS, (pr + 1) * PAIR_ROWS)
                upd = lax.dot_general(v[:, vl], k_dec[:, kl], (((0,), (0,)), ((), ())),
                                      preferred_element_type=F32)
                upd_ref[j, c, pr] = jnp.where(same_head, upd, 0.0)
        ctx[j]["decays"] = decays

    def stage_gmlp(j):
        va = p_ref[j, :, OFF_VA:OFF_VA + A_WIDTH]
        vn = (va * _rms_scale(va) * nv_ref[...]).astype(BF16)
        for blk in range(rows // A_BLOCK):
            r0 = blk * A_BLOCK
            for hf in range(2):
                c0 = hf * half_width
                v_blk = vn[r0:r0 + A_BLOCK, c0:c0 + half_width]
                v_stack = jnp.concatenate(
                    [jnp.where(lane_group == g, v_blk, jnp.zeros_like(v_blk))
                     for g in range(half_groups)], axis=0)
                z = (jnp.dot(ws_half[hf], v_stack, preferred_element_type=F32)
                     + bs_ref[:, c0:c0 + half_width])
                u = p_ref[j, r0:r0 + A_BLOCK, OFF_U + c0:OFF_U + c0 + half_width]
                y_ref[j, r0:r0 + A_BLOCK, c0:c0 + half_width] = (u * z).astype(BF16)

    def stage_recurrence(j):
        decays = ctx[j].pop("decays")
        for pr in range(N_PAIRS):
            kl = slice(pr * PAIR_LANES, (pr + 1) * PAIR_LANES)
            st = st_ref[j, pr]
            for c in range(n_chunks):
                st = st * decays[c][:, kl] + upd_ref[j, c, pr]
                stb_ref[j, c, pr] = st.astype(BF16).T
            st_ref[j, pr] = st

    def stage_readout(j):
        for c in range(n_chunks):
            t0 = c * CHUNK
            for pr in range(N_PAIRS):
                q = (p_ref[j, t0:t0 + CHUNK,
                           OFF_Q + pr * PAIR_LANES:OFF_Q + (pr + 1) * PAIR_LANES]
                     * (B_HEAD_K ** -0.5)).astype(BF16)
                o = jnp.dot(q, stb_ref[j, c, pr], preferred_element_type=F32)
                p_ref[j, t0:t0 + CHUNK,
                      OFF_VB + pr * PAIR_ROWS:OFF_VB + (pr + 1) * PAIR_ROWS] = o

    def stage_out_a(j):
        ctx[j]["out_a"] = jnp.dot(y_ref[j, :, :A_WIDTH], wout_ref[:A_WIDTH, :].astype(BF16),
                                  preferred_element_type=F32)

    def stage_attention_gate(j):
        for hd in range(B_HEADS):
            cols = slice(hd * B_HEAD_V, (hd + 1) * B_HEAD_V)
            o = p_ref[j, :, OFF_VB + hd * B_HEAD_V:OFF_VB + (hd + 1) * B_HEAD_V]
            o = o * _rms_scale(o) * no_ref[:, cols]
            g = p_ref[j, :, OFF_G + hd * B_HEAD_V:OFF_G + (hd + 1) * B_HEAD_V]
            y_ref[j, :, A_WIDTH + hd * B_HEAD_V:A_WIDTH + (hd + 1) * B_HEAD_V] = (
                o * (g * _sigmoid(g))).astype(BF16)

    def stage_out_b(j):
        out_b = jnp.dot(y_ref[j, :, A_WIDTH:], wout_ref[A_WIDTH:, :].astype(BF16),
                        preferred_element_type=F32)
        o_ref[j] = x_ref[j] + (ctx[j].pop("out_a") + out_b)

    head = (stage_norm, stage_rank_columns, stage_attention_columns, stage_gate,
            stage_gmlp_columns, stage_cumulative_decay)
    for stage in head:
        for j in range(n_seqs):
            stage(j)
    for j in range(n_seqs):
        stage_updates(j)
        stage_gate_columns(j)
    for j in range(n_seqs):
        stage_gmlp(j)
        stage_recurrence(j)
    for j in range(n_seqs):
        stage_out_a(j)
        stage_readout(j)
    for stage in (stage_attention_gate, stage_out_b):
        for j in range(n_seqs):
            stage(j)


def mixer_call(x, norm_w, w_in_p, norm_v, w_s, b_s_full, w_gk_p, b_gk, norm_o, w_out, *, layer):
    bsz, seq, _ = x.shape
    n_chunks = MIX_ROWS // CHUNK
    return pl.pallas_call(
        _mixer_kernel,
        out_shape=jax.ShapeDtypeStruct(x.shape, F32),
        grid=(bsz // MIX_SEQS, seq // MIX_ROWS),
        in_specs=[
            pl.BlockSpec((MIX_SEQS, MIX_ROWS, D_MODEL), lambda b, s: (b, s, 0)),
            _resident((1, D_MODEL)),
            _resident(w_in_p.shape),
            _resident((1, A_WIDTH)),
            _resident(w_s.shape),
            _resident(b_s_full.shape),
            _resident(w_gk_p.shape),
            _resident((1, B_KEY)),
            _resident((1, B_WIDTH)),
            _layer_resident(w_out.shape, layer),
        ],
        out_specs=pl.BlockSpec((MIX_SEQS, MIX_ROWS, D_MODEL), lambda b, s: (b, s, 0)),
        scratch_shapes=[
            pltpu.VMEM((MIX_SEQS, MIX_ROWS, D_MODEL), BF16),
            pltpu.VMEM((MIX_SEQS, MIX_ROWS, IN_COLS_PAD), F32),
            pltpu.VMEM((MIX_SEQS, MIX_ROWS, D_MODEL), BF16),
            pltpu.VMEM((MIX_SEQS, n_chunks, N_PAIRS, PAIR_ROWS, PAIR_LANES), F32),
            pltpu.VMEM((MIX_SEQS, n_chunks, N_PAIRS, PAIR_LANES, PAIR_ROWS), BF16),
            pltpu.VMEM((MIX_SEQS, N_PAIRS, PAIR_ROWS, PAIR_LANES), F32),
        ],
        compiler_params=pltpu.CompilerParams(
            dimension_semantics=("arbitrary", "arbitrary"), vmem_limit_bytes=VMEM_LIMIT),
        name="mixer",
    )(x, norm_w, w_in_p, norm_v, w_s, b_s_full, w_gk_p, b_gk, norm_o, w_out)


def _regroup_w_in(w):
    a2, bk, bw = 2 * A_WIDTH, B_KEY, B_WIDTH
    qkv = w[:, a2:a2 + 2 * bk + bw]
    g = w[:, a2 + 2 * bk + bw:a2 + 2 * bk + 2 * bw]
    r = jnp.pad(w[:, a2 + 2 * bk + 2 * bw:], ((0, 0), (0, R_PAD - GATE_RANK)))
    return jnp.concatenate([r, qkv, w[:, :a2], g], axis=1).astype(BF16)


def kernel(x, ff1_norm, ff1_w_in, ff1_w_out, mix_norm, w_in, gmlp_norm_v, gmlp_w_s, gmlp_b_s,
           gla_w_gk2, gla_b_gk, gla_norm_o, w_out, ff2_norm, ff2_w_in, ff2_w_out, final_norm):
    bsz, seq, d = x.shape
    depth = ff1_norm.shape[0]
    row = lambda v: v.reshape(1, -1)
    rows3 = lambda v: v.reshape(depth, 1, -1)
    fw = row(final_norm)
    ff1_nw, ff2_nw = rows3(ff1_norm), rows3(ff2_norm)
    for l in range(depth):
        w_in_p = _regroup_w_in(w_in[l])
        w_gk_p = jnp.pad(gla_w_gk2[l], ((0, R_PAD - GATE_RANK), (0, 0))).astype(BF16)
        b_s_full = jnp.repeat(gmlp_b_s[l].T, A_GROUP_DIM, axis=1)

        x = ffn_call(x.reshape(bsz * seq, d), ff1_nw, ff1_w_in, ff1_w_out, fw,
                     layer=l, final_norm=False).reshape(bsz, seq, d)
        x = mixer_call(x, row(mix_norm[l]), w_in_p, row(gmlp_norm_v[l]), gmlp_w_s[l], b_s_full,
                       w_gk_p, row(gla_b_gk[l]), row(gla_norm_o[l]), w_out, layer=l)
        x = ffn_call(x.reshape(bsz * seq, d), ff2_nw, ff2_w_in, ff2_w_out, fw,
                     layer=l, final_norm=(l == depth - 1)).reshape(bsz, seq, d)
    return x
```

```python
import functools

import jax
import jax.numpy as jnp
from jax import lax
from jax.experimental import pallas as pl
from jax.experimental.pallas import tpu as pltpu

D_MODEL = 1024
D_FF = 2816
CHUNK = 64
A_WIDTH = 512
A_GROUPS = 8
A_GROUP_DIM = A_WIDTH // A_GROUPS
A_BLOCK = 128
B_WIDTH = 512
B_HEADS = 4
B_HEAD_V = 128
B_HEAD_K = 64
B_KEY = B_HEADS * B_HEAD_K
GATE_RANK = 16
GATE_NORM = 16.0
EPS = 1e-6

LANES = 128
R_PAD = LANES
OFF_R = 0
OFF_Q = OFF_R + R_PAD
OFF_K = OFF_Q + B_KEY
OFF_VB = OFF_K + B_KEY
OFF_U = OFF_VB + B_WIDTH
OFF_VA = OFF_U + A_WIDTH
OFF_G = OFF_VA + A_WIDTH
IN_COLS_PAD = OFF_G + B_WIDTH
SRC_COLS = 2 * A_WIDTH + 2 * B_KEY + 2 * B_WIDTH
COLUMN_MOVES = ((OFF_Q, 2 * A_WIDTH, 2 * B_KEY + B_WIDTH),
                (OFF_U, 0, 2 * A_WIDTH),
                (OFF_G, 2 * A_WIDTH + 2 * B_KEY + B_WIDTH, B_WIDTH))

FF_CHUNK = 256
N_FF_CHUNKS = D_FF // FF_CHUNK
N_STAGES = 8
STAGE_IN_ROWS = D_MODEL // N_STAGES
STAGE_OUT_ROWS = D_FF // N_STAGES
MIX_STAGE_ROWS = D_MODEL // N_STAGES
FFN_ROWS = 1024
FFN_SUB = 512
MIX_ROWS = 512
MIX_SEQS = 2
VMEM_LIMIT = 56 * 1024 * 1024

PAIR_ROWS = 2 * B_HEAD_V
PAIR_LANES = 2 * B_HEAD_K
N_PAIRS = B_HEADS // 2

BF16 = jnp.bfloat16
F32 = jnp.float32


def _rms_scale(x):
    return lax.rsqrt(jnp.mean(x * x, axis=-1, keepdims=True) + EPS)


def _sigmoid(x):
    return 1.0 / (1.0 + jnp.exp(-x))


def _split3(x):
    hi = x.astype(BF16)
    r1 = x - hi.astype(F32)
    mid = r1.astype(BF16)
    lo = (r1 - mid.astype(F32)).astype(BF16)
    return hi, mid, lo


def _ffn_kernel(x_ref, nw_ref, win_hbm, wout_hbm, fw_ref, o_ref, xn_ref, acc_ref, win_ref,
                wout_ref, stage_in, stage_out, sem, *, layer, final_norm):
    def stage_copies(k, slot):
        rows_in = pl.ds(k * STAGE_IN_ROWS, STAGE_IN_ROWS)
        rows_out = pl.ds(k * STAGE_OUT_ROWS, STAGE_OUT_ROWS)
        return (pltpu.make_async_copy(win_hbm.at[layer, rows_in, :], stage_in.at[slot],
                                      sem.at[slot, 0]),
                pltpu.make_async_copy(wout_hbm.at[layer, rows_out, :], stage_out.at[slot],
                                      sem.at[slot, 1]))

    @pl.when(pl.program_id(0) == 0)
    def _():
        for cp in stage_copies(0, 0):
            cp.start()
        for k in range(N_STAGES):
            slot = k % 2
            if k + 1 < N_STAGES:
                for cp in stage_copies(k + 1, 1 - slot):
                    cp.start()
            for cp in stage_copies(k, slot):
                cp.wait()
            win_ref[k * STAGE_IN_ROWS:(k + 1) * STAGE_IN_ROWS, :] = stage_in[slot].astype(BF16)
            wout_ref[k * STAGE_OUT_ROWS:(k + 1) * STAGE_OUT_ROWS, :] = (
                stage_out[slot].astype(BF16))

    subs = [slice(s * FFN_SUB, (s + 1) * FFN_SUB) for s in range(FFN_ROWS // FFN_SUB)]

    scales = []
    for rows in subs:
        x = x_ref[rows, :]
        xn_ref[rows, :] = (x * nw_ref[...]).astype(BF16)
        scales.append(_rms_scale(x))

    def up(item):
        s, c = item
        lo = c * FF_CHUNK
        xn = xn_ref[subs[s], :]
        a = jnp.dot(xn, win_ref[:, lo:lo + FF_CHUNK], preferred_element_type=F32)
        g = jnp.dot(xn, win_ref[:, D_FF + lo:D_FF + lo + FF_CHUNK], preferred_element_type=F32)
        return a, g

    items = [(s, c) for s in range(len(subs)) for c in range(N_FF_CHUNKS)]
    nxt = up(items[0])
    for k, (s, c) in enumerate(items):
        lo = c * FF_CHUNK
        rows = subs[s]
        a, g = nxt
        if k + 1 < len(items):
            nxt = up(items[k + 1])
        g = scales[s] * g
        hh = (g * _sigmoid(g) * (scales[s] * a)).astype(BF16)
        part = jnp.dot(hh, wout_ref[lo:lo + FF_CHUNK, :], preferred_element_type=F32)
        if c == 0:
            acc_ref[...] = part
        elif c < N_FF_CHUNKS - 1:
            acc_ref[...] += part
        else:
            y = x_ref[rows, :] + 0.5 * (acc_ref[...] + part)
            if final_norm:
                y = y * _rms_scale(y) * fw_ref[...]
            o_ref[rows, :] = y


def _resident(shape):
    zeros = (0,) * len(shape)
    return pl.BlockSpec(shape, lambda *_: zeros, pipeline_mode=pl.Buffered(1))


def _layer_resident(shape, layer):
    index = (layer,) + (0,) * (len(shape) - 1)
    return pl.BlockSpec((None,) + tuple(shape[1:]), lambda *_: index,
                        pipeline_mode=pl.Buffered(1))


def ffn_call(x2d, norm_w, w_in, w_out, final_w, *, layer, final_norm):
    n = x2d.shape[0]
    return pl.pallas_call(
        functools.partial(_ffn_kernel, layer=layer, final_norm=final_norm),
        out_shape=jax.ShapeDtypeStruct(x2d.shape, F32),
        grid=(n // FFN_ROWS,),
        in_specs=[
            pl.BlockSpec((FFN_ROWS, D_MODEL), lambda i: (i, 0)),
            _layer_resident(norm_w.shape, layer),
            pl.BlockSpec(memory_space=pl.ANY),
            pl.BlockSpec(memory_space=pl.ANY),
            _resident((1, D_MODEL)),
        ],
        out_specs=pl.BlockSpec((FFN_ROWS, D_MODEL), lambda i: (i, 0)),
        scratch_shapes=[
            pltpu.VMEM((FFN_ROWS, D_MODEL), BF16),
            pltpu.VMEM((FFN_SUB, D_MODEL), F32),
            pltpu.VMEM((D_MODEL, 2 * D_FF), BF16),
            pltpu.VMEM((D_FF, D_MODEL), BF16),
            pltpu.VMEM((2, STAGE_IN_ROWS, 2 * D_FF), F32),
            pltpu.VMEM((2, STAGE_OUT_ROWS, D_MODEL), F32),
            pltpu.SemaphoreType.DMA((2, 2)),
        ],
        compiler_params=pltpu.CompilerParams(
            dimension_semantics=("arbitrary",), vmem_limit_bytes=VMEM_LIMIT),
        name="ffn",
    )(x2d, norm_w, w_in, w_out, final_w)


def _mixer_kernel(x_ref, nw_ref, win_hbm, wr_ref, nv_ref, ws_ref, bs_ref, wgk_ref, bgk_ref, no_ref,
                  wout_ref, o_ref, h_ref, p_ref, y_ref, upd_ref, stb_ref, st_ref, win_ref,
                  stage_ref, sem, *, layer):
    n_seqs, rows = x_ref.shape[0], x_ref.shape[1]
    n_chunks = rows // CHUNK

    @pl.when(pl.program_id(1) == 0)
    def _():
        st_ref[...] = jnp.zeros_like(st_ref)

    def stage_copy(k, slot):
        src = win_hbm.at[layer, pl.ds(k * MIX_STAGE_ROWS, MIX_STAGE_ROWS), pl.ds(0, SRC_COLS)]
        return pltpu.make_async_copy(src, stage_ref.at[slot], sem.at[slot])

    @pl.when((pl.program_id(0) == 0) & (pl.program_id(1) == 0))
    def _():
        stage_copy(0, 0).start()
        for k in range(N_STAGES):
            slot = k % 2
            if k + 1 < N_STAGES:
                stage_copy(k + 1, 1 - slot).start()
            stage_copy(k, slot).wait()
            rws = slice(k * MIX_STAGE_ROWS, (k + 1) * MIX_STAGE_ROWS)
            for dst, src, width in COLUMN_MOVES:
                win_ref[rws, dst - R_PAD:dst - R_PAD + width] = (
                    stage_ref[slot, :, src:src + width].astype(BF16))

    tri = (lax.broadcasted_iota(jnp.int32, (CHUNK, CHUNK), 0)
           >= lax.broadcasted_iota(jnp.int32, (CHUNK, CHUNK), 1)).astype(BF16)
    same_head = (lax.broadcasted_iota(jnp.int32, (PAIR_ROWS, PAIR_LANES), 0) // B_HEAD_V
                 == lax.broadcasted_iota(jnp.int32, (PAIR_ROWS, PAIR_LANES), 1) // B_HEAD_K)
    t_chunk = lax.broadcasted_iota(jnp.int32, (A_BLOCK, A_BLOCK), 0) // CHUNK
    s_chunk = lax.broadcasted_iota(jnp.int32, (A_BLOCK, A_BLOCK), 1) // CHUNK
    causal = t_chunk >= s_chunk
    half_groups = A_GROUPS // 2
    half_width = A_WIDTH // 2
    ws_half = [
        jnp.concatenate([jnp.where(causal, ws_ref[hf * half_groups + g], 0.0).astype(BF16)
                         for g in range(half_groups)], axis=1)
        for hf in range(2)]
    lane_group = lax.broadcasted_iota(jnp.int32, (A_BLOCK, half_width), 1) // A_GROUP_DIM

    ctx = [dict() for _ in range(n_seqs)]

    def project(j, lo, hi):
        p_ref[j, :, lo:hi] = ctx[j]["scale"] * jnp.dot(
            h_ref[j], win_ref[:, lo - R_PAD:hi - R_PAD], preferred_element_type=F32)

    def stage_norm(j):
        x = x_ref[j]
        h_ref[j] = (x * nw_ref[...]).astype(BF16)
        ctx[j]["scale"] = _rms_scale(x)

    def stage_rank_columns(j):
        p_ref[j, :, OFF_R:OFF_Q] = ctx[j]["scale"] * jnp.dot(
            h_ref[j], wr_ref[...], preferred_element_type=F32)

    def stage_attention_columns(j):
        project(j, OFF_Q, OFF_U)

    def stage_gate(j):
        r = p_ref[j, :, OFF_R:OFF_R + R_PAD].astype(BF16)
        ctx[j]["gate_pre"] = jnp.dot(r, wgk_ref[...], preferred_element_type=F32) + bgk_ref[...]

    def stage_gmlp_columns(j):
        project(j, OFF_U, OFF_G)

    def stage_cumulative_decay(j):
        gate_pre = ctx[j].pop("gate_pre")
        log_a = (jnp.minimum(gate_pre, 0.0) - jnp.log1p(jnp.exp(-jnp.abs(gate_pre)))) / GATE_NORM
        ctx[j]["cum3"] = [
            jnp.dot(tri, jnp.concatenate(_split3(log_a[c * CHUNK:(c + 1) * CHUNK, :]), axis=1),
                    preferred_element_type=F32) for c in range(n_chunks)]

    def stage_gate_columns(j):
        project(j, OFF_G, IN_COLS_PAD)

    def stage_updates(j):
        cum3 = ctx[j].pop("cum3")
        decays = []
        for c in range(n_chunks):
            t0 = c * CHUNK
            cum = cum3[c][:, :B_KEY] + cum3[c][:, B_KEY:2 * B_KEY] + cum3[c][:, 2 * B_KEY:]
            tot = cum[CHUNK - 1:CHUNK, :]
            decays.append(jnp.exp(tot))
            k_dec = (p_ref[j, t0:t0 + CHUNK, OFF_K:OFF_K + B_KEY]
                     * jnp.exp(tot - cum)).astype(BF16)
            v = p_ref[j, t0:t0 + CHUNK, OFF_VB:OFF_VB + B_WIDTH].astype(BF16)
            for pr in range(N_PAIRS):
                kl = slice(pr * PAIR_LANES, (pr + 1) * PAIR_LANES)
                vl = slice(pr * PAIR_ROWS, (pr + 1) * PAIR_ROWS)
                upd = lax.dot_general(v[:, vl], k_dec[:, kl], (((0,), (0,)), ((), ())),
                                      preferred_element_type=F32)
                upd_ref[j, c, pr] = jnp.where(same_head, upd, 0.0)
        ctx[j]["decays"] = decays

    def stage_gmlp(j):
        va = p_ref[j, :, OFF_VA:OFF_VA + A_WIDTH]
        vn = (va * _rms_scale(va) * nv_ref[...]).astype(BF16)
        for blk in range(rows // A_BLOCK):
            r0 = blk * A_BLOCK
            for hf in range(2):
                c0 = hf * half_width
                v_blk = vn[r0:r0 + A_BLOCK, c0:c0 + half_width]
                v_stack = jnp.concatenate(
                    [jnp.where(lane_group == g, v_blk, jnp.zeros_like(v_blk))
                     for g in range(half_groups)], axis=0)
                z = (jnp.dot(ws_half[hf], v_stack, preferred_element_type=F32)
                     + bs_ref[:, c0:c0 + half_width])
                u = p_ref[j, r0:r0 + A_BLOCK, OFF_U + c0:OFF_U + c0 + half_width]
                y_ref[j, r0:r0 + A_BLOCK, c0:c0 + half_width] = (u * z).astype(BF16)

    def stage_recurrence(j):
        decays = ctx[j].pop("decays")
        for pr in range(N_PAIRS):
            kl = slice(pr * PAIR_LANES, (pr + 1) * PAIR_LANES)
            st = st_ref[j, pr]
            for c in range(n_chunks):
                st = st * decays[c][:, kl] + upd_ref[j, c, pr]
                stb_ref[j, c, pr] = st.astype(BF16).T
            st_ref[j, pr] = st

    def stage_readout(j):
        for c in range(n_chunks):
            t0 = c * CHUNK
            for pr in range(N_PAIRS):
                q = (p_ref[j, t0:t0 + CHUNK,
                           OFF_Q + pr * PAIR_LANES:OFF_Q + (pr + 1) * PAIR_LANES]
                     * (B_HEAD_K ** -0.5)).astype(BF16)
                o = jnp.dot(q, stb_ref[j, c, pr], preferred_element_type=F32)
                p_ref[j, t0:t0 + CHUNK,
                      OFF_VB + pr * PAIR_ROWS:OFF_VB + (pr + 1) * PAIR_ROWS] = o

    def stage_out_a(j):
        ctx[j]["out_a"] = jnp.dot(y_ref[j, :, :A_WIDTH], wout_ref[:A_WIDTH, :].astype(BF16),
                                  preferred_element_type=F32)

    def stage_attention_gate(j):
        for hd in range(B_HEADS):
            cols = slice(hd * B_HEAD_V, (hd + 1) * B_HEAD_V)
            o = p_ref[j, :, OFF_VB + hd * B_HEAD_V:OFF_VB + (hd + 1) * B_HEAD_V]
            o = o * _rms_scale(o) * no_ref[:, cols]
            g = p_ref[j, :, OFF_G + hd * B_HEAD_V:OFF_G + (hd + 1) * B_HEAD_V]
            y_ref[j, :, A_WIDTH + hd * B_HEAD_V:A_WIDTH + (hd + 1) * B_HEAD_V] = (
                o * (g * _sigmoid(g))).astype(BF16)

    def stage_out_b(j):
        out_b = jnp.dot(y_ref[j, :, A_WIDTH:], wout_ref[A_WIDTH:, :].astype(BF16),
                        preferred_element_type=F32)
        o_ref[j] = x_ref[j] + (ctx[j].pop("out_a") + out_b)

    head = (stage_norm, stage_rank_columns, stage_attention_columns, stage_gate,
            stage_gmlp_columns, stage_cumulative_decay)
    for stage in head:
        for j in range(n_seqs):
            stage(j)
    for j in range(n_seqs):
        stage_updates(j)
        stage_gate_columns(j)
    for j in range(n_seqs):
        stage_gmlp(j)
        stage_recurrence(j)
    for j in range(n_seqs):
        stage_out_a(j)
        stage_readout(j)
    for stage in (stage_attention_gate, stage_out_b):
        for j in range(n_seqs):
            stage(j)


def mixer_call(x, norm_w, w_in, w_r_p, norm_v, w_s, b_s_full, w_gk_p, b_gk, norm_o, w_out, *,
               layer):
    bsz, seq, _ = x.shape
    n_chunks = MIX_ROWS // CHUNK
    return pl.pallas_call(
        functools.partial(_mixer_kernel, layer=layer),
        out_shape=jax.ShapeDtypeStruct(x.shape, F32),
        grid=(bsz // MIX_SEQS, seq // MIX_ROWS),
        in_specs=[
            pl.BlockSpec((MIX_SEQS, MIX_ROWS, D_MODEL), lambda b, s: (b, s, 0)),
            _resident((1, D_MODEL)),
            pl.BlockSpec(memory_space=pl.ANY),
            _resident(w_r_p.shape),
            _resident((1, A_WIDTH)),
            _resident(w_s.shape),
            _resident(b_s_full.shape),
            _resident(w_gk_p.shape),
            _resident((1, B_KEY)),
            _resident((1, B_WIDTH)),
            _layer_resident(w_out.shape, layer),
        ],
        out_specs=pl.BlockSpec((MIX_SEQS, MIX_ROWS, D_MODEL), lambda b, s: (b, s, 0)),
        scratch_shapes=[
            pltpu.VMEM((MIX_SEQS, MIX_ROWS, D_MODEL), BF16),
            pltpu.VMEM((MIX_SEQS, MIX_ROWS, IN_COLS_PAD), F32),
            pltpu.VMEM((MIX_SEQS, MIX_ROWS, D_MODEL), BF16),
            pltpu.VMEM((MIX_SEQS, n_chunks, N_PAIRS, PAIR_ROWS, PAIR_LANES), F32),
            pltpu.VMEM((MIX_SEQS, n_chunks, N_PAIRS, PAIR_LANES, PAIR_ROWS), BF16),
            pltpu.VMEM((MIX_SEQS, N_PAIRS, PAIR_ROWS, PAIR_LANES), F32),
            pltpu.VMEM((D_MODEL, SRC_COLS), BF16),
            pltpu.VMEM((2, MIX_STAGE_ROWS, SRC_COLS), F32),
            pltpu.SemaphoreType.DMA((2,)),
        ],
        compiler_params=pltpu.CompilerParams(
            dimension_semantics=("arbitrary", "arbitrary"), vmem_limit_bytes=VMEM_LIMIT),
        name="mixer",
    )(x, norm_w, w_in, w_r_p, norm_v, w_s, b_s_full, w_gk_p, b_gk, norm_o, w_out)


def kernel(x, ff1_norm, ff1_w_in, ff1_w_out, mix_norm, w_in, gmlp_norm_v, gmlp_w_s, gmlp_b_s,
           gla_w_gk2, gla_b_gk, gla_norm_o, w_out, ff2_norm, ff2_w_in, ff2_w_out, final_norm):
    bsz, seq, d = x.shape
    depth = ff1_norm.shape[0]
    row = lambda v: v.reshape(1, -1)
    rows3 = lambda v: v.reshape(depth, 1, -1)
    fw = row(final_norm)
    ff1_nw, ff2_nw = rows3(ff1_norm), rows3(ff2_norm)
    for l in range(depth):
        w_r_p = jnp.pad(w_in[l][:, SRC_COLS:], ((0, 0), (0, R_PAD - GATE_RANK))).astype(BF16)
        w_gk_p = jnp.pad(gla_w_gk2[l], ((0, R_PAD - GATE_RANK), (0, 0))).astype(BF16)
        b_s_full = jnp.repeat(gmlp_b_s[l].T, A_GROUP_DIM, axis=1)

        x = ffn_call(x.reshape(bsz * seq, d), ff1_nw, ff1_w_in, ff1_w_out, fw,
                     layer=l, final_norm=False).reshape(bsz, seq, d)
        x = mixer_call(x, row(mix_norm[l]), w_in, w_r_p, row(gmlp_norm_v[l]), gmlp_w_s[l],
                       b_s_full, w_gk_p, row(gla_b_gk[l]), row(gla_norm_o[l]), w_out, layer=l)
        x = ffn_call(x.reshape(bsz * seq, d), ff2_nw, ff2_w_in, ff2_w_out, fw,
                     layer=l, final_norm=(l == depth - 1)).reshape(bsz, seq, d)
    return x
```

```python
import functools

import jax
import jax.numpy as jnp
from jax import lax
from jax.experimental import pallas as pl
from jax.experimental.pallas import tpu as pltpu

D_MODEL = 1024
D_FF = 2816
CHUNK = 64
A_WIDTH = 512
A_GROUPS = 8
A_GROUP_DIM = A_WIDTH // A_GROUPS
A_BLOCK = 128
B_WIDTH = 512
B_HEADS = 4
B_HEAD_V = 128
B_HEAD_K = 64
B_KEY = B_HEADS * B_HEAD_K
GATE_RANK = 16
GATE_NORM = 16.0
EPS = 1e-6

LANES = 128
R_PAD = LANES
OFF_R = 0
OFF_Q = OFF_R + R_PAD
OFF_K = OFF_Q + B_KEY
OFF_VB = OFF_K + B_KEY
OFF_U = OFF_VB + B_WIDTH
OFF_VA = OFF_U + A_WIDTH
OFF_G = OFF_VA + A_WIDTH
IN_COLS_PAD = OFF_G + B_WIDTH

FF_CHUNK = 256
N_FF_CHUNKS = D_FF // FF_CHUNK
FFN_ROWS = 1024
FFN_SUB = 512
MIX_ROWS = 512
MIX_SEQS = 2
VMEM_LIMIT = 56 * 1024 * 1024

PAIR_ROWS = 2 * B_HEAD_V
PAIR_LANES = 2 * B_HEAD_K
N_PAIRS = B_HEADS // 2

BF16 = jnp.bfloat16
F32 = jnp.float32


def _rms_scale(x):
    return lax.rsqrt(jnp.mean(x * x, axis=-1, keepdims=True) + EPS)


def _sigmoid(x):
    return 1.0 / (1.0 + jnp.exp(-x))


def _split3(x):
    hi = x.astype(BF16)
    r1 = x - hi.astype(F32)
    mid = r1.astype(BF16)
    lo = (r1 - mid.astype(F32)).astype(BF16)
    return hi, mid, lo


def _ffn_kernel(x_ref, nw_ref, win_hbm, wout_hbm, fw_ref, o_ref, xn_ref, acc_ref, win_ref,
                wout_ref, stage_in, stage_out, sem, *, layer, final_norm):
    def chunk_copies(c):
        slot = c % 2
        lo = c * FF_CHUNK
        return (pltpu.make_async_copy(win_hbm.at[layer, :, pl.ds(lo, FF_CHUNK)],
                                      stage_in.at[slot, 0], sem.at[slot, 0]),
                pltpu.make_async_copy(win_hbm.at[layer, :, pl.ds(D_FF + lo, FF_CHUNK)],
                                      stage_in.at[slot, 1], sem.at[slot, 1]),
                pltpu.make_async_copy(wout_hbm.at[layer, pl.ds(lo, FF_CHUNK), :],
                                      stage_out.at[slot], sem.at[slot, 2]))

    def receive_chunk(c):
        slot = c % 2
        lo = c * FF_CHUNK
        for cp in chunk_copies(c):
            cp.wait()
        win_ref[:, lo:lo + FF_CHUNK] = stage_in[slot, 0].astype(BF16)
        win_ref[:, D_FF + lo:D_FF + lo + FF_CHUNK] = stage_in[slot, 1].astype(BF16)
        wout_ref[lo:lo + FF_CHUNK, :] = stage_out[slot].astype(BF16)
        if c + 2 < N_FF_CHUNKS:
            for cp in chunk_copies(c + 2):
                cp.start()

    subs = [slice(s * FFN_SUB, (s + 1) * FFN_SUB) for s in range(FFN_ROWS // FFN_SUB)]

    def body(load_weights):
        if load_weights:
            for c in range(2):
                for cp in chunk_copies(c):
                    cp.start()

        scales = []
        for rows in subs:
            x = x_ref[rows, :]
            xn_ref[rows, :] = (x * nw_ref[...]).astype(BF16)
            scales.append(_rms_scale(x))

        def up(item):
            s, c = item
            if load_weights and s == 0:
                receive_chunk(c)
            lo = c * FF_CHUNK
            xn = xn_ref[subs[s], :]
            a = jnp.dot(xn, win_ref[:, lo:lo + FF_CHUNK], preferred_element_type=F32)
            g = jnp.dot(xn, win_ref[:, D_FF + lo:D_FF + lo + FF_CHUNK],
                        preferred_element_type=F32)
            return a, g

        items = [(s, c) for s in range(len(subs)) for c in range(N_FF_CHUNKS)]
        nxt = up(items[0])
        for k, (s, c) in enumerate(items):
            lo = c * FF_CHUNK
            rows = subs[s]
            a, g = nxt
            if k + 1 < len(items):
                nxt = up(items[k + 1])
            g = scales[s] * g
            hh = (g * _sigmoid(g) * (scales[s] * a)).astype(BF16)
            part = jnp.dot(hh, wout_ref[lo:lo + FF_CHUNK, :], preferred_element_type=F32)
            if c == 0:
                acc_ref[...] = part
            elif c < N_FF_CHUNKS - 1:
                acc_ref[...] += part
            else:
                y = x_ref[rows, :] + 0.5 * (acc_ref[...] + part)
                if final_norm:
                    y = y * _rms_scale(y) * fw_ref[...]
                o_ref[rows, :] = y

    lax.cond(pl.program_id(0) == 0, lambda: body(True), lambda: body(False))


def _resident(shape):
    zeros = (0,) * len(shape)
    return pl.BlockSpec(shape, lambda *_: zeros, pipeline_mode=pl.Buffered(1))


def _layer_resident(shape, layer):
    index = (layer,) + (0,) * (len(shape) - 1)
    return pl.BlockSpec((None,) + tuple(shape[1:]), lambda *_: index,
                        pipeline_mode=pl.Buffered(1))


def ffn_call(x2d, norm_w, w_in, w_out, final_w, *, layer, final_norm):
    n = x2d.shape[0]
    return pl.pallas_call(
        functools.partial(_ffn_kernel, layer=layer, final_norm=final_norm),
        out_shape=jax.ShapeDtypeStruct(x2d.shape, F32),
        grid=(n // FFN_ROWS,),
        in_specs=[
            pl.BlockSpec((FFN_ROWS, D_MODEL), lambda i: (i, 0)),
            _layer_resident(norm_w.shape, layer),
            pl.BlockSpec(memory_space=pl.ANY),
            pl.BlockSpec(memory_space=pl.ANY),
            _resident((1, D_MODEL)),
        ],
        out_specs=pl.BlockSpec((FFN_ROWS, D_MODEL), lambda i: (i, 0)),
        scratch_shapes=[
            pltpu.VMEM((FFN_ROWS, D_MODEL), BF16),
            pltpu.VMEM((FFN_SUB, D_MODEL), F32),
            pltpu.VMEM((D_MODEL, 2 * D_FF), BF16),
            pltpu.VMEM((D_FF, D_MODEL), BF16),
            pltpu.VMEM((2, 2, D_MODEL, FF_CHUNK), F32),
            pltpu.VMEM((2, FF_CHUNK, D_MODEL), F32),
            pltpu.SemaphoreType.DMA((2, 3)),
        ],
        compiler_params=pltpu.CompilerParams(
            dimension_semantics=("arbitrary",), vmem_limit_bytes=VMEM_LIMIT),
        name="ffn",
    )(x2d, norm_w, w_in, w_out, final_w)


def _mixer_kernel(x_ref, nw_ref, win_ref, nv_ref, ws_ref, bs_ref, wgk_ref, bgk_ref, no_ref,
                  wout_ref, o_ref, h_ref, p_ref, y_ref, upd_ref, stb_ref, st_ref):
    n_seqs, rows = x_ref.shape[0], x_ref.shape[1]
    n_chunks = rows // CHUNK

    @pl.when(pl.program_id(1) == 0)
    def _():
        st_ref[...] = jnp.zeros_like(st_ref)

    tri = (lax.broadcasted_iota(jnp.int32, (CHUNK, CHUNK), 0)
           >= lax.broadcasted_iota(jnp.int32, (CHUNK, CHUNK), 1)).astype(BF16)
    same_head = (lax.broadcasted_iota(jnp.int32, (PAIR_ROWS, PAIR_LANES), 0) // B_HEAD_V
                 == lax.broadcasted_iota(jnp.int32, (PAIR_ROWS, PAIR_LANES), 1) // B_HEAD_K)
    t_chunk = lax.broadcasted_iota(jnp.int32, (A_BLOCK, A_BLOCK), 0) // CHUNK
    s_chunk = lax.broadcasted_iota(jnp.int32, (A_BLOCK, A_BLOCK), 1) // CHUNK
    causal = t_chunk >= s_chunk
    half_groups = A_GROUPS // 2
    half_width = A_WIDTH // 2
    ws_half = [
        jnp.concatenate([jnp.where(causal, ws_ref[hf * half_groups + g], 0.0).astype(BF16)
                         for g in range(half_groups)], axis=1)
        for hf in range(2)]
    lane_group = lax.broadcasted_iota(jnp.int32, (A_BLOCK, half_width), 1) // A_GROUP_DIM

    ctx = [dict() for _ in range(n_seqs)]

    def project(j, lo, hi):
        p_ref[j, :, lo:hi] = ctx[j]["scale"] * jnp.dot(
            h_ref[j], win_ref[:, lo:hi], preferred_element_type=F32)

    def stage_norm(j):
        x = x_ref[j]
        h_ref[j] = (x * nw_ref[...]).astype(BF16)
        ctx[j]["scale"] = _rms_scale(x)

    def stage_rank_columns(j):
        project(j, OFF_R, OFF_Q)

    def stage_attention_columns(j):
        project(j, OFF_Q, OFF_U)

    def stage_gate(j):
        r = p_ref[j, :, OFF_R:OFF_R + R_PAD].astype(BF16)
        ctx[j]["gate_pre"] = jnp.dot(r, wgk_ref[...], preferred_element_type=F32) + bgk_ref[...]

    def stage_gmlp_columns(j):
        project(j, OFF_U, OFF_G)

    def stage_cumulative_decay(j):
        gate_pre = ctx[j].pop("gate_pre")
        log_a = (jnp.minimum(gate_pre, 0.0) - jnp.log1p(jnp.exp(-jnp.abs(gate_pre)))) / GATE_NORM
        ctx[j]["cum3"] = [
            jnp.dot(tri, jnp.concatenate(_split3(log_a[c * CHUNK:(c + 1) * CHUNK, :]), axis=1),
                    preferred_element_type=F32) for c in range(n_chunks)]

    def stage_gate_columns(j):
        project(j, OFF_G, IN_COLS_PAD)

    def stage_updates(j):
        cum3 = ctx[j].pop("cum3")
        decays = []
        for c in range(n_chunks):
            t0 = c * CHUNK
            cum = cum3[c][:, :B_KEY] + cum3[c][:, B_KEY:2 * B_KEY] + cum3[c][:, 2 * B_KEY:]
            tot = cum[CHUNK - 1:CHUNK, :]
            decays.append(jnp.exp(tot))
            k_dec = (p_ref[j, t0:t0 + CHUNK, OFF_K:OFF_K + B_KEY]
                     * jnp.exp(tot - cum)).astype(BF16)
            v = p_ref[j, t0:t0 + CHUNK, OFF_VB:OFF_VB + B_WIDTH].astype(BF16)
            for pr in range(N_PAIRS):
                kl = slice(pr * PAIR_LANES, (pr + 1) * PAIR_LANES)
                vl = slice(pr * PAIR_ROWS, (pr + 1) * PAIR_ROWS)
                upd = lax.dot_general(v[:, vl], k_dec[:, kl], (((0,), (0,)), ((), ())),
                                      preferred_element_type=F32)
                upd_ref[j, c, pr] = jnp.where(same_head, upd, 0.0)
        ctx[j]["decays"] = decays

    def stage_gmlp(j):
        va = p_ref[j, :, OFF_VA:OFF_VA + A_WIDTH]
        vn = (va * _rms_scale(va) * nv_ref[...]).astype(BF16)
        for blk in range(rows // A_BLOCK):
            r0 = blk * A_BLOCK
            for hf in range(2):
                c0 = hf * half_width
                v_blk = vn[r0:r0 + A_BLOCK, c0:c0 + half_width]
                v_stack = jnp.concatenate(
                    [jnp.where(lane_group == g, v_blk, jnp.zeros_like(v_blk))
                     for g in range(half_groups)], axis=0)
                z = (jnp.dot(ws_half[hf], v_stack, preferred_element_type=F32)
                     + bs_ref[:, c0:c0 + half_width])
                u = p_ref[j, r0:r0 + A_BLOCK, OFF_U + c0:OFF_U + c0 + half_width]
                y_ref[j, r0:r0 + A_BLOCK, c0:c0 + half_width] = (u * z).astype(BF16)

    def stage_recurrence(j):
        decays = ctx[j].pop("decays")
        for pr in range(N_PAIRS):
            kl = slice(pr * PAIR_LANES, (pr + 1) * PAIR_LANES)
            st = st_ref[j, pr]
            for c in range(n_chunks):
                st = st * decays[c][:, kl] + upd_ref[j, c, pr]
                stb_ref[j, c, pr] = st.astype(BF16).T
            st_ref[j, pr] = st

    def stage_readout(j):
        for c in range(n_chunks):
            t0 = c * CHUNK
            for pr in range(N_PAIRS):
                q = (p_ref[j, t0:t0 + CHUNK,
                           OFF_Q + pr * PAIR_LANES:OFF_Q + (pr + 1) * PAIR_LANES]
                     * (B_HEAD_K ** -0.5)).astype(BF16)
                o = jnp.dot(q, stb_ref[j, c, pr], preferred_element_type=F32)
                p_ref[j, t0:t0 + CHUNK,
                      OFF_VB + pr * PAIR_ROWS:OFF_VB + (pr + 1) * PAIR_ROWS] = o

    def stage_out_a(j):
        ctx[j]["out_a"] = jnp.dot(y_ref[j, :, :A_WIDTH], wout_ref[:A_WIDTH, :].astype(BF16),
                                  preferred_element_type=F32)

    def stage_attention_gate(j):
        for hd in range(B_HEADS):
            cols = slice(hd * B_HEAD_V, (hd + 1) * B_HEAD_V)
            o = p_ref[j, :, OFF_VB + hd * B_HEAD_V:OFF_VB + (hd + 1) * B_HEAD_V]
            o = o * _rms_scale(o) * no_ref[:, cols]
            g = p_ref[j, :, OFF_G + hd * B_HEAD_V:OFF_G + (hd + 1) * B_HEAD_V]
            y_ref[j, :, A_WIDTH + hd * B_HEAD_V:A_WIDTH + (hd + 1) * B_HEAD_V] = (
                o * (g * _sigmoid(g))).astype(BF16)

    def stage_out_b(j):
        out_b = jnp.dot(y_ref[j, :, A_WIDTH:], wout_ref[A_WIDTH:, :].astype(BF16),
                        preferred_element_type=F32)
        o_ref[j] = x_ref[j] + (ctx[j].pop("out_a") + out_b)

    head = (stage_norm, stage_rank_columns, stage_attention_columns, stage_gate,
            stage_gmlp_columns, stage_cumulative_decay)
    for stage in head:
        for j in range(n_seqs):
            stage(j)
    for j in range(n_seqs):
        stage_updates(j)
        stage_gate_columns(j)
    for j in range(n_seqs):
        stage_gmlp(j)
        stage_recurrence(j)
    for j in range(n_seqs):
        stage_out_a(j)
        stage_readout(j)
    for stage in (stage_attention_gate, stage_out_b):
        for j in range(n_seqs):
            stage(j)


def mixer_call(x, norm_w, w_in_p, norm_v, w_s, b_s_full, w_gk_p, b_gk, norm_o, w_out, *, layer):
    bsz, seq, _ = x.shape
    n_chunks = MIX_ROWS // CHUNK
    return pl.pallas_call(
        _mixer_kernel,
        out_shape=jax.ShapeDtypeStruct(x.shape, F32),
        grid=(bsz // MIX_SEQS, seq // MIX_ROWS),
        in_specs=[
            pl.BlockSpec((MIX_SEQS, MIX_ROWS, D_MODEL), lambda b, s: (b, s, 0)),
            _resident((1, D_MODEL)),
            _resident(w_in_p.shape),
            _resident((1, A_WIDTH)),
            _resident(w_s.shape),
            _resident(b_s_full.shape),
            _resident(w_gk_p.shape),
            _resident((1, B_KEY)),
            _resident((1, B_WIDTH)),
            _layer_resident(w_out.shape, layer),
        ],
        out_specs=pl.BlockSpec((MIX_SEQS, MIX_ROWS, D_MODEL), lambda b, s: (b, s, 0)),
        scratch_shapes=[
            pltpu.VMEM((MIX_SEQS, MIX_ROWS, D_MODEL), BF16),
            pltpu.VMEM((MIX_SEQS, MIX_ROWS, IN_COLS_PAD), F32),
            pltpu.VMEM((MIX_SEQS, MIX_ROWS, D_MODEL), BF16),
            pltpu.VMEM((MIX_SEQS, n_chunks, N_PAIRS, PAIR_ROWS, PAIR_LANES), F32),
            pltpu.VMEM((MIX_SEQS, n_chunks, N_PAIRS, PAIR_LANES, PAIR_ROWS), BF16),
            pltpu.VMEM((MIX_SEQS, N_PAIRS, PAIR_ROWS, PAIR_LANES), F32),
        ],
        compiler_params=pltpu.CompilerParams(
            dimension_semantics=("arbitrary", "arbitrary"), vmem_limit_bytes=VMEM_LIMIT),
        name="mixer",
    )(x, norm_w, w_in_p, norm_v, w_s, b_s_full, w_gk_p, b_gk, norm_o, w_out)


def _regroup_w_in(w):
    a2, bk, bw = 2 * A_WIDTH, B_KEY, B_WIDTH
    qkv = w[:, a2:a2 + 2 * bk + bw]
    g = w[:, a2 + 2 * bk + bw:a2 + 2 * bk + 2 * bw]
    r = jnp.pad(w[:, a2 + 2 * bk + 2 * bw:], ((0, 0), (0, R_PAD - GATE_RANK)))
    return jnp.concatenate([r, qkv, w[:, :a2], g], axis=1).astype(BF16)


def kernel(x, ff1_norm, ff1_w_in, ff1_w_out, mix_norm, w_in, gmlp_norm_v, gmlp_w_s, gmlp_b_s,
           gla_w_gk2, gla_b_gk, gla_norm_o, w_out, ff2_norm, ff2_w_in, ff2_w_out, final_norm):
    bsz, seq, d = x.shape
    depth = ff1_norm.shape[0]
    row = lambda v: v.reshape(1, -1)
    rows3 = lambda v: v.reshape(depth, 1, -1)
    fw = row(final_norm)
    ff1_nw, ff2_nw = rows3(ff1_norm), rows3(ff2_norm)
    for l in range(depth):
        w_in_p = _regroup_w_in(w_in[l])
        w_gk_p = jnp.pad(gla_w_gk2[l], ((0, R_PAD - GATE_RANK), (0, 0))).astype(BF16)
        b_s_full = jnp.repeat(gmlp_b_s[l].T, A_GROUP_DIM, axis=1)

        x = ffn_call(x.reshape(bsz * seq, d), ff1_nw, ff1_w_in, ff1_w_out, fw,
                     layer=l, final_norm=False).reshape(bsz, seq, d)
        x = mixer_call(x, row(mix_norm[l]), w_in_p, row(gmlp_norm_v[l]), gmlp_w_s[l], b_s_full,
                       w_gk_p, row(gla_b_gk[l]), row(gla_norm_o[l]), w_out, layer=l)
        x = ffn_call(x.reshape(bsz * seq, d), ff2_nw, ff2_w_in, ff2_w_out, fw,
                     layer=l, final_norm=(l == depth - 1)).reshape(bsz, seq, d)
    return x
```

```python
import functools

import jax
import jax.numpy as jnp
from jax import lax
from jax.experimental import pallas as pl
from jax.experimental.pallas import tpu as pltpu

D_MODEL = 1024
D_FF = 2816
CHUNK = 64
A_WIDTH = 512
A_GROUPS = 8
A_GROUP_DIM = A_WIDTH // A_GROUPS
A_BLOCK = 128
B_WIDTH = 512
B_HEADS = 4
B_HEAD_V = 128
B_HEAD_K = 64
B_KEY = B_HEADS * B_HEAD_K
GATE_RANK = 16
GATE_NORM = 16.0
EPS = 1e-6

LANES = 128
R_PAD = LANES
OFF_R = 0
OFF_Q = OFF_R + R_PAD
OFF_K = OFF_Q + B_KEY
OFF_VB = OFF_K + B_KEY
OFF_U = OFF_VB + B_WIDTH
OFF_VA = OFF_U + A_WIDTH
OFF_G = OFF_VA + A_WIDTH
IN_COLS_PAD = OFF_G + B_WIDTH

FF_CHUNK = 256
N_FF_CHUNKS = D_FF // FF_CHUNK
N_STAGES = 8
STAGE_IN_ROWS = D_MODEL // N_STAGES
STAGE_OUT_ROWS = D_FF // N_STAGES
FFN_ROWS = 1024
FFN_SUB = 256
MIX_ROWS = 512
MIX_SEQS = 2
VMEM_LIMIT = 56 * 1024 * 1024

PAIR_ROWS = 2 * B_HEAD_V
PAIR_LANES = 2 * B_HEAD_K
N_PAIRS = B_HEADS // 2

BF16 = jnp.bfloat16
F32 = jnp.float32


def _rms_scale(x):
    return lax.rsqrt(jnp.mean(x * x, axis=-1, keepdims=True) + EPS)


def _sigmoid(x):
    return 1.0 / (1.0 + jnp.exp(-x))


def _split3(x):
    hi = x.astype(BF16)
    r1 = x - hi.astype(F32)
    mid = r1.astype(BF16)
    lo = (r1 - mid.astype(F32)).astype(BF16)
    return hi, mid, lo


def _ffn_kernel(x_ref, nw_ref, win_hbm, wout_hbm, fw_ref, o_ref, xn_ref, acc_ref, win_ref,
                wout_ref, stage_in, stage_out, sem, *, layer, final_norm):
    def stage_copies(k, slot):
        rows_in = pl.ds(k * STAGE_IN_ROWS, STAGE_IN_ROWS)
        rows_out = pl.ds(k * STAGE_OUT_ROWS, STAGE_OUT_ROWS)
        return (pltpu.make_async_copy(win_hbm.at[layer, rows_in, :], stage_in.at[slot],
                                      sem.at[slot, 0]),
                pltpu.make_async_copy(wout_hbm.at[layer, rows_out, :], stage_out.at[slot],
                                      sem.at[slot, 1]))

    @pl.when(pl.program_id(0) == 0)
    def _():
        for cp in stage_copies(0, 0):
            cp.start()
        for k in range(N_STAGES):
            slot = k % 2
            if k + 1 < N_STAGES:
                for cp in stage_copies(k + 1, 1 - slot):
                    cp.start()
            for cp in stage_copies(k, slot):
                cp.wait()
            win_ref[k * STAGE_IN_ROWS:(k + 1) * STAGE_IN_ROWS, :] = stage_in[slot].astype(BF16)
            wout_ref[k * STAGE_OUT_ROWS:(k + 1) * STAGE_OUT_ROWS, :] = (
                stage_out[slot].astype(BF16))

    subs = [slice(s * FFN_SUB, (s + 1) * FFN_SUB) for s in range(FFN_ROWS // FFN_SUB)]

    scales = []
    for rows in subs:
        x = x_ref[rows, :]
        xn_ref[rows, :] = (x * nw_ref[...]).astype(BF16)
        scales.append(_rms_scale(x))

    def up(item):
        s, c = item
        lo = c * FF_CHUNK
        xn = xn_ref[subs[s], :]
        a = jnp.dot(xn, win_ref[:, lo:lo + FF_CHUNK], preferred_element_type=F32)
        g = jnp.dot(xn, win_ref[:, D_FF + lo:D_FF + lo + FF_CHUNK], preferred_element_type=F32)
        return a, g

    items = [(s, c) for s in range(len(subs)) for c in range(N_FF_CHUNKS)]
    nxt = up(items[0])
    for k, (s, c) in enumerate(items):
        lo = c * FF_CHUNK
        rows = subs[s]
        a, g = nxt
        if k + 1 < len(items):
            nxt = up(items[k + 1])
        g = scales[s] * g
        hh = (g * _sigmoid(g) * (scales[s] * a)).astype(BF16)
        part = jnp.dot(hh, wout_ref[lo:lo + FF_CHUNK, :], preferred_element_type=F32)
        if c == 0:
            acc_ref[...] = part
        elif c < N_FF_CHUNKS - 1:
            acc_ref[...] += part
        else:
            y = x_ref[rows, :] + 0.5 * (acc_ref[...] + part)
            if final_norm:
                y = y * _rms_scale(y) * fw_ref[...]
            o_ref[rows, :] = y


def _resident(shape):
    zeros = (0,) * len(shape)
    return pl.BlockSpec(shape, lambda *_: zeros, pipeline_mode=pl.Buffered(1))


def _layer_resident(shape, layer):
    index = (layer,) + (0,) * (len(shape) - 1)
    return pl.BlockSpec((None,) + tuple(shape[1:]), lambda *_: index,
                        pipeline_mode=pl.Buffered(1))


def ffn_call(x2d, norm_w, w_in, w_out, final_w, *, layer, final_norm):
    n = x2d.shape[0]
    return pl.pallas_call(
        functools.partial(_ffn_kernel, layer=layer, final_norm=final_norm),
        out_shape=jax.ShapeDtypeStruct(x2d.shape, F32),
        grid=(n // FFN_ROWS,),
        in_specs=[
            pl.BlockSpec((FFN_ROWS, D_MODEL), lambda i: (i, 0)),
            _layer_resident(norm_w.shape, layer),
            pl.BlockSpec(memory_space=pl.ANY),
            pl.BlockSpec(memory_space=pl.ANY),
            _resident((1, D_MODEL)),
        ],
        out_specs=pl.BlockSpec((FFN_ROWS, D_MODEL), lambda i: (i, 0)),
        scratch_shapes=[
            pltpu.VMEM((FFN_ROWS, D_MODEL), BF16),
            pltpu.VMEM((FFN_SUB, D_MODEL), F32),
            pltpu.VMEM((D_MODEL, 2 * D_FF), BF16),
            pltpu.VMEM((D_FF, D_MODEL), BF16),
            pltpu.VMEM((2, STAGE_IN_ROWS, 2 * D_FF), F32),
            pltpu.VMEM((2, STAGE_OUT_ROWS, D_MODEL), F32),
            pltpu.SemaphoreType.DMA((2, 2)),
        ],
        compiler_params=pltpu.CompilerParams(
            dimension_semantics=("arbitrary",), vmem_limit_bytes=VMEM_LIMIT),
        name="ffn",
    )(x2d, norm_w, w_in, w_out, final_w)


def _mixer_kernel(x_ref, nw_ref, win_ref, nv_ref, ws_ref, bs_ref, wgk_ref, bgk_ref, no_ref,
                  wout_ref, o_ref, h_ref, p_ref, y_ref, upd_ref, stb_ref, st_ref):
    n_seqs, rows = x_ref.shape[0], x_ref.shape[1]
    n_chunks = rows // CHUNK

    @pl.when(pl.program_id(1) == 0)
    def _():
        st_ref[...] = jnp.zeros_like(st_ref)

    tri = (lax.broadcasted_iota(jnp.int32, (CHUNK, CHUNK), 0)
           >= lax.broadcasted_iota(jnp.int32, (CHUNK, CHUNK), 1)).astype(BF16)
    same_head = (lax.broadcasted_iota(jnp.int32, (PAIR_ROWS, PAIR_LANES), 0) // B_HEAD_V
                 == lax.broadcasted_iota(jnp.int32, (PAIR_ROWS, PAIR_LANES), 1) // B_HEAD_K)
    t_chunk = lax.broadcasted_iota(jnp.int32, (A_BLOCK, A_BLOCK), 0) // CHUNK
    s_chunk = lax.broadcasted_iota(jnp.int32, (A_BLOCK, A_BLOCK), 1) // CHUNK
    causal = t_chunk >= s_chunk
    half_groups = A_GROUPS // 2
    half_width = A_WIDTH // 2
    ws_half = [
        jnp.concatenate([jnp.where(causal, ws_ref[hf * half_groups + g], 0.0).astype(BF16)
                         for g in range(half_groups)], axis=1)
        for hf in range(2)]
    lane_group = lax.broadcasted_iota(jnp.int32, (A_BLOCK, half_width), 1) // A_GROUP_DIM

    ctx = [dict() for _ in range(n_seqs)]

    def project(j, lo, hi):
        p_ref[j, :, lo:hi] = ctx[j]["scale"] * jnp.dot(
            h_ref[j], win_ref[:, lo:hi], preferred_element_type=F32)

    def stage_norm(j):
        x = x_ref[j]
        h_ref[j] = (x * nw_ref[...]).astype(BF16)
        ctx[j]["scale"] = _rms_scale(x)

    def stage_rank_columns(j):
        project(j, OFF_R, OFF_Q)

    def stage_attention_columns(j):
        project(j, OFF_Q, OFF_U)

    def stage_gate(j):
        r = p_ref[j, :, OFF_R:OFF_R + R_PAD].astype(BF16)
        ctx[j]["gate_pre"] = jnp.dot(r, wgk_ref[...], preferred_element_type=F32) + bgk_ref[...]

    def stage_gmlp_columns(j):
        project(j, OFF_U, OFF_G)

    def stage_cumulative_decay(j):
        gate_pre = ctx[j].pop("gate_pre")
        log_a = (jnp.minimum(gate_pre, 0.0) - jnp.log1p(jnp.exp(-jnp.abs(gate_pre)))) / GATE_NORM
        ctx[j]["cum3"] = [
            jnp.dot(tri, jnp.concatenate(_split3(log_a[c * CHUNK:(c + 1) * CHUNK, :]), axis=1),
                    preferred_element_type=F32) for c in range(n_chunks)]

    def stage_gate_columns(j):
        project(j, OFF_G, IN_COLS_PAD)

    def stage_updates(j):
        cum3 = ctx[j].pop("cum3")
        decays = []
        for c in range(n_chunks):
            t0 = c * CHUNK
            cum = cum3[c][:, :B_KEY] + cum3[c][:, B_KEY:2 * B_KEY] + cum3[c][:, 2 * B_KEY:]
            tot = cum[CHUNK - 1:CHUNK, :]
            decays.append(jnp.exp(tot))
            k_dec = (p_ref[j, t0:t0 + CHUNK, OFF_K:OFF_K + B_KEY]
                     * jnp.exp(tot - cum)).astype(BF16)
            v = p_ref[j, t0:t0 + CHUNK, OFF_VB:OFF_VB + B_WIDTH].astype(BF16)
            for pr in range(N_PAIRS):
                kl = slice(pr * PAIR_LANES, (pr + 1) * PAIR_LANES)
                vl = slice(pr * PAIR_ROWS, (pr + 1) * PAIR_ROWS)
                upd = lax.dot_general(v[:, vl], k_dec[:, kl], (((0,), (0,)), ((), ())),
                                      preferred_element_type=F32)
                upd_ref[j, c, pr] = jnp.where(same_head, upd, 0.0)
        ctx[j]["decays"] = decays

    def stage_gmlp(j):
        va = p_ref[j, :, OFF_VA:OFF_VA + A_WIDTH]
        vn = (va * _rms_scale(va) * nv_ref[...]).astype(BF16)
        for blk in range(rows // A_BLOCK):
            r0 = blk * A_BLOCK
            for hf in range(2):
                c0 = hf * half_width
                v_blk = vn[r0:r0 + A_BLOCK, c0:c0 + half_width]
                v_stack = jnp.concatenate(
                    [jnp.where(lane_group == g, v_blk, jnp.zeros_like(v_blk))
                     for g in range(half_groups)], axis=0)
                z = (jnp.dot(ws_half[hf], v_stack, preferred_element_type=F32)
                     + bs_ref[:, c0:c0 + half_width])
                u = p_ref[j, r0:r0 + A_BLOCK, OFF_U + c0:OFF_U + c0 + half_width]
                y_ref[j, r0:r0 + A_BLOCK, c0:c0 + half_width] = (u * z).astype(BF16)

    def stage_recurrence(j):
        decays = ctx[j].pop("decays")
        for pr in range(N_PAIRS):
            kl = slice(pr * PAIR_LANES, (pr + 1) * PAIR_LANES)
            st = st_ref[j, pr]
            for c in range(n_chunks):
                st = st * decays[c][:, kl] + upd_ref[j, c, pr]
                stb_ref[j, c, pr] = st.astype(BF16).T
            st_ref[j, pr] = st

    def stage_readout(j):
        for c in range(n_chunks):
            t0 = c * CHUNK
            for pr in range(N_PAIRS):
                q = (p_ref[j, t0:t0 + CHUNK,
                           OFF_Q + pr * PAIR_LANES:OFF_Q + (pr + 1) * PAIR_LANES]
                     * (B_HEAD_K ** -0.5)).astype(BF16)
                o = jnp.dot(q, stb_ref[j, c, pr], preferred_element_type=F32)
                p_ref[j, t0:t0 + CHUNK,
                      OFF_VB + pr * PAIR_ROWS:OFF_VB + (pr + 1) * PAIR_ROWS] = o

    def stage_out_a(j):
        ctx[j]["out_a"] = jnp.dot(y_ref[j, :, :A_WIDTH], wout_ref[:A_WIDTH, :].astype(BF16),
                                  preferred_element_type=F32)

    def stage_attention_gate(j):
        for hd in range(B_HEADS):
            cols = slice(hd * B_HEAD_V, (hd + 1) * B_HEAD_V)
            o = p_ref[j, :, OFF_VB + hd * B_HEAD_V:OFF_VB + (hd + 1) * B_HEAD_V]
            o = o * _rms_scale(o) * no_ref[:, cols]
            g = p_ref[j, :, OFF_G + hd * B_HEAD_V:OFF_G + (hd + 1) * B_HEAD_V]
            y_ref[j, :, A_WIDTH + hd * B_HEAD_V:A_WIDTH + (hd + 1) * B_HEAD_V] = (
                o * (g * _sigmoid(g))).astype(BF16)

    def stage_out_b(j):
        out_b = jnp.dot(y_ref[j, :, A_WIDTH:], wout_ref[A_WIDTH:, :].astype(BF16),
                        preferred_element_type=F32)
        o_ref[j] = x_ref[j] + (ctx[j].pop("out_a") + out_b)

    head = (stage_norm, stage_rank_columns, stage_attention_columns, stage_gate,
            stage_gmlp_columns, stage_cumulative_decay)
    for stage in head:
        for j in range(n_seqs):
            stage(j)
    for j in range(n_seqs):
        stage_updates(j)
        stage_gate_columns(j)
    for j in range(n_seqs):
        stage_gmlp(j)
        stage_recurrence(j)
    for j in range(n_seqs):
        stage_out_a(j)
        stage_readout(j)
    for stage in (stage_attention_gate, stage_out_b):
        for j in range(n_seqs):
            stage(j)


def mixer_call(x, norm_w, w_in_p, norm_v, w_s, b_s_full, w_gk_p, b_gk, norm_o, w_out, *, layer):
    bsz, seq, _ = x.shape
    n_chunks = MIX_ROWS // CHUNK
    return pl.pallas_call(
        _mixer_kernel,
        out_shape=jax.ShapeDtypeStruct(x.shape, F32),
        grid=(bsz // MIX_SEQS, seq // MIX_ROWS),
        in_specs=[
            pl.BlockSpec((MIX_SEQS, MIX_ROWS, D_MODEL), lambda b, s: (b, s, 0)),
            _resident((1, D_MODEL)),
            _resident(w_in_p.shape),
            _resident((1, A_WIDTH)),
            _resident(w_s.shape),
            _resident(b_s_full.shape),
            _resident(w_gk_p.shape),
            _resident((1, B_KEY)),
            _resident((1, B_WIDTH)),
            _layer_resident(w_out.shape, layer),
        ],
        out_specs=pl.BlockSpec((MIX_SEQS, MIX_ROWS, D_MODEL), lambda b, s: (b, s, 0)),
        scratch_shapes=[
            pltpu.VMEM((MIX_SEQS, MIX_ROWS, D_MODEL), BF16),
            pltpu.VMEM((MIX_SEQS, MIX_ROWS, IN_COLS_PAD), F32),
            pltpu.VMEM((MIX_SEQS, MIX_ROWS, D_MODEL), BF16),
            pltpu.VMEM((MIX_SEQS, n_chunks, N_PAIRS, PAIR_ROWS, PAIR_LANES), F32),
            pltpu.VMEM((MIX_SEQS, n_chunks, N_PAIRS, PAIR_LANES, PAIR_ROWS), BF16),
            pltpu.VMEM((MIX_SEQS, N_PAIRS, PAIR_ROWS, PAIR_LANES), F32),
        ],
        compiler_params=pltpu.CompilerParams(
            dimension_semantics=("arbitrary", "arbitrary"), vmem_limit_bytes=VMEM_LIMIT),
        name="mixer",
    )(x, norm_w, w_in_p, norm_v, w_s, b_s_full, w_gk_p, b_gk, norm_o, w_out)


def _regroup_w_in(w):
    a2, bk, bw = 2 * A_WIDTH, B_KEY, B_WIDTH
    qkv = w[:, a2:a2 + 2 * bk + bw]
    g = w[:, a2 + 2 * bk + bw:a2 + 2 * bk + 2 * bw]
    r = jnp.pad(w[:, a2 + 2 * bk + 2 * bw:], ((0, 0), (0, R_PAD - GATE_RANK)))
    return jnp.concatenate([r, qkv, w[:, :a2], g], axis=1).astype(BF16)


def kernel(x, ff1_norm, ff1_w_in, ff1_w_out, mix_norm, w_in, gmlp_norm_v, gmlp_w_s, gmlp_b_s,
           gla_w_gk2, gla_b_gk, gla_norm_o, w_out, ff2_norm, ff2_w_in, ff2_w_out, final_norm):
    bsz, seq, d = x.shape
    depth = ff1_norm.shape[0]
    row = lambda v: v.reshape(1, -1)
    rows3 = lambda v: v.reshape(depth, 1, -1)
    fw = row(final_norm)
    ff1_nw, ff2_nw = rows3(ff1_norm), rows3(ff2_norm)
    for l in range(depth):
        w_in_p = _regroup_w_in(w_in[l])
        w_gk_p = jnp.pad(gla_w_gk2[l], ((0, R_PAD - GATE_RANK), (0, 0))).astype(BF16)
        b_s_full = jnp.repeat(gmlp_b_s[l].T, A_GROUP_DIM, axis=1)

        x = ffn_call(x.reshape(bsz * seq, d), ff1_nw, ff1_w_in, ff1_w_out, fw,
                     layer=l, final_norm=False).reshape(bsz, seq, d)
        x = mixer_call(x, row(mix_norm[l]), w_in_p, row(gmlp_norm_v[l]), gmlp_w_s[l], b_s_full,
                       w_gk_p, row(gla_b_gk[l]), row(gla_norm_o[l]), w_out, layer=l)
        x = ffn_call(x.reshape(bsz * seq, d), ff2_nw, ff2_w_in, ff2_w_out, fw,
                     layer=l, final_norm=(l == depth - 1)).reshape(bsz, seq, d)
    return x
```

```python
import functools

import jax
import jax.numpy as jnp
from jax import lax
from jax.experimental import pallas as pl
from jax.experimental.pallas import tpu as pltpu

D_MODEL = 1024
D_FF = 2816
CHUNK = 64
A_WIDTH = 512
A_GROUPS = 8
A_GROUP_DIM = A_WIDTH // A_GROUPS
A_BLOCK = 128
B_WIDTH = 512
B_HEADS = 4
B_HEAD_V = 128
B_HEAD_K = 64
B_KEY = B_HEADS * B_HEAD_K
GATE_RANK = 16
GATE_NORM = 16.0
EPS = 1e-6

LANES = 128
R_PAD = LANES
OFF_R = 0
OFF_Q = OFF_R + R_PAD
OFF_K = OFF_Q + B_KEY
OFF_VB = OFF_K + B_KEY
OFF_U = OFF_VB + B_WIDTH
OFF_VA = OFF_U + A_WIDTH
OFF_G = OFF_VA + A_WIDTH
IN_COLS_PAD = OFF_G + B_WIDTH

FF_CHUNK = 256
N_FF_CHUNKS = D_FF // FF_CHUNK
N_STAGES = 8
STAGE_IN_ROWS = D_MODEL // N_STAGES
STAGE_OUT_ROWS = D_FF // N_STAGES
FFN_ROWS = 1024
FFN_SUB = 128
MIX_ROWS = 512
MIX_SEQS = 2
VMEM_LIMIT = 56 * 1024 * 1024

PAIR_ROWS = 2 * B_HEAD_V
PAIR_LANES = 2 * B_HEAD_K
N_PAIRS = B_HEADS // 2

BF16 = jnp.bfloat16
F32 = jnp.float32


def _rms_scale(x):
    return lax.rsqrt(jnp.mean(x * x, axis=-1, keepdims=True) + EPS)


def _sigmoid(x):
    return 1.0 / (1.0 + jnp.exp(-x))


def _split3(x):
    hi = x.astype(BF16)
    r1 = x - hi.astype(F32)
    mid = r1.astype(BF16)
    lo = (r1 - mid.astype(F32)).astype(BF16)
    return hi, mid, lo


def _ffn_kernel(x_ref, nw_ref, win_hbm, wout_hbm, fw_ref, o_ref, xn_ref, acc_ref, win_ref,
                wout_ref, stage_in, stage_out, sem, *, layer, final_norm):
    def stage_copies(k, slot):
        rows_in = pl.ds(k * STAGE_IN_ROWS, STAGE_IN_ROWS)
        rows_out = pl.ds(k * STAGE_OUT_ROWS, STAGE_OUT_ROWS)
        return (pltpu.make_async_copy(win_hbm.at[layer, rows_in, :], stage_in.at[slot],
                                      sem.at[slot, 0]),
                pltpu.make_async_copy(wout_hbm.at[layer, rows_out, :], stage_out.at[slot],
                                      sem.at[slot, 1]))

    @pl.when(pl.program_id(0) == 0)
    def _():
        for cp in stage_copies(0, 0):
            cp.start()
        for k in range(N_STAGES):
            slot = k % 2
            if k + 1 < N_STAGES:
                for cp in stage_copies(k + 1, 1 - slot):
                    cp.start()
            for cp in stage_copies(k, slot):
                cp.wait()
            win_ref[k * STAGE_IN_ROWS:(k + 1) * STAGE_IN_ROWS, :] = stage_in[slot].astype(BF16)
            wout_ref[k * STAGE_OUT_ROWS:(k + 1) * STAGE_OUT_ROWS, :] = (
                stage_out[slot].astype(BF16))

    subs = [slice(s * FFN_SUB, (s + 1) * FFN_SUB) for s in range(FFN_ROWS // FFN_SUB)]

    scales = []
    for rows in subs:
        x = x_ref[rows, :]
        xn_ref[rows, :] = (x * nw_ref[...]).astype(BF16)
        scales.append(_rms_scale(x))

    def up(item):
        s, c = item
        lo = c * FF_CHUNK
        xn = xn_ref[subs[s], :]
        a = jnp.dot(xn, win_ref[:, lo:lo + FF_CHUNK], preferred_element_type=F32)
        g = jnp.dot(xn, win_ref[:, D_FF + lo:D_FF + lo + FF_CHUNK], preferred_element_type=F32)
        return a, g

    items = [(s, c) for s in range(len(subs)) for c in range(N_FF_CHUNKS)]
    nxt = up(items[0])
    for k, (s, c) in enumerate(items):
        lo = c * FF_CHUNK
        rows = subs[s]
        a, g = nxt
        if k + 1 < len(items):
            nxt = up(items[k + 1])
        g = scales[s] * g
        hh = (g * _sigmoid(g) * (scales[s] * a)).astype(BF16)
        part = jnp.dot(hh, wout_ref[lo:lo + FF_CHUNK, :], preferred_element_type=F32)
        if c == 0:
            acc_ref[...] = part
        elif c < N_FF_CHUNKS - 1:
            acc_ref[...] += part
        else:
            y = x_ref[rows, :] + 0.5 * (acc_ref[...] + part)
            if final_norm:
                y = y * _rms_scale(y) * fw_ref[...]
            o_ref[rows, :] = y


def _resident(shape):
    zeros = (0,) * len(shape)
    return pl.BlockSpec(shape, lambda *_: zeros, pipeline_mode=pl.Buffered(1))


def _layer_resident(shape, layer):
    index = (layer,) + (0,) * (len(shape) - 1)
    return pl.BlockSpec((None,) + tuple(shape[1:]), lambda *_: index,
                        pipeline_mode=pl.Buffered(1))


def ffn_call(x2d, norm_w, w_in, w_out, final_w, *, layer, final_norm):
    n = x2d.shape[0]
    return pl.pallas_call(
        functools.partial(_ffn_kernel, layer=layer, final_norm=final_norm),
        out_shape=jax.ShapeDtypeStruct(x2d.shape, F32),
        grid=(n // FFN_ROWS,),
        in_specs=[
            pl.BlockSpec((FFN_ROWS, D_MODEL), lambda i: (i, 0)),
            _layer_resident(norm_w.shape, layer),
            pl.BlockSpec(memory_space=pl.ANY),
            pl.BlockSpec(memory_space=pl.ANY),
            _resident((1, D_MODEL)),
        ],
        out_specs=pl.BlockSpec((FFN_ROWS, D_MODEL), lambda i: (i, 0)),
        scratch_shapes=[
            pltpu.VMEM((FFN_ROWS, D_MODEL), BF16),
            pltpu.VMEM((FFN_SUB, D_MODEL), F32),
            pltpu.VMEM((D_MODEL, 2 * D_FF), BF16),
            pltpu.VMEM((D_FF, D_MODEL), BF16),
            pltpu.VMEM((2, STAGE_IN_ROWS, 2 * D_FF), F32),
            pltpu.VMEM((2, STAGE_OUT_ROWS, D_MODEL), F32),
            pltpu.SemaphoreType.DMA((2, 2)),
        ],
        compiler_params=pltpu.CompilerParams(
            dimension_semantics=("arbitrary",), vmem_limit_bytes=VMEM_LIMIT),
        name="ffn",
    )(x2d, norm_w, w_in, w_out, final_w)


def _mixer_kernel(x_ref, nw_ref, win_ref, nv_ref, ws_ref, bs_ref, wgk_ref, bgk_ref, no_ref,
                  wout_ref, o_ref, h_ref, p_ref, y_ref, upd_ref, stb_ref, st_ref):
    n_seqs, rows = x_ref.shape[0], x_ref.shape[1]
    n_chunks = rows // CHUNK

    @pl.when(pl.program_id(1) == 0)
    def _():
        st_ref[...] = jnp.zeros_like(st_ref)

    tri = (lax.broadcasted_iota(jnp.int32, (CHUNK, CHUNK), 0)
           >= lax.broadcasted_iota(jnp.int32, (CHUNK, CHUNK), 1)).astype(BF16)
    same_head = (lax.broadcasted_iota(jnp.int32, (PAIR_ROWS, PAIR_LANES), 0) // B_HEAD_V
                 == lax.broadcasted_iota(jnp.int32, (PAIR_ROWS, PAIR_LANES), 1) // B_HEAD_K)
    t_chunk = lax.broadcasted_iota(jnp.int32, (A_BLOCK, A_BLOCK), 0) // CHUNK
    s_chunk = lax.broadcasted_iota(jnp.int32, (A_BLOCK, A_BLOCK), 1) // CHUNK
    causal = t_chunk >= s_chunk
    half_groups = A_GROUPS // 2
    half_width = A_WIDTH // 2
    ws_half = [
        jnp.concatenate([jnp.where(causal, ws_ref[hf * half_groups + g], 0.0).astype(BF16)
                         for g in range(half_groups)], axis=1)
        for hf in range(2)]
    lane_group = lax.broadcasted_iota(jnp.int32, (A_BLOCK, half_width), 1) // A_GROUP_DIM

    ctx = [dict() for _ in range(n_seqs)]

    def project(j, lo, hi):
        p_ref[j, :, lo:hi] = ctx[j]["scale"] * jnp.dot(
            h_ref[j], win_ref[:, lo:hi], preferred_element_type=F32)

    def stage_norm(j):
        x = x_ref[j]
        h_ref[j] = (x * nw_ref[...]).astype(BF16)
        ctx[j]["scale"] = _rms_scale(x)

    def stage_rank_columns(j):
        project(j, OFF_R, OFF_Q)

    def stage_attention_columns(j):
        project(j, OFF_Q, OFF_U)

    def stage_gate(j):
        r = p_ref[j, :, OFF_R:OFF_R + R_PAD].astype(BF16)
        ctx[j]["gate_pre"] = jnp.dot(r, wgk_ref[...], preferred_element_type=F32) + bgk_ref[...]

    def stage_gmlp_columns(j):
        project(j, OFF_U, OFF_G)

    def stage_cumulative_decay(j):
        gate_pre = ctx[j].pop("gate_pre")
        log_a = (jnp.minimum(gate_pre, 0.0) - jnp.log1p(jnp.exp(-jnp.abs(gate_pre)))) / GATE_NORM
        cums = []
        for c in range(n_chunks):
            cum3 = jnp.dot(tri, jnp.concatenate(_split3(log_a[c * CHUNK:(c + 1) * CHUNK, :]),
                                                axis=1), preferred_element_type=F32)
            cums.append(cum3[:, :B_KEY] + cum3[:, B_KEY:2 * B_KEY] + cum3[:, 2 * B_KEY:])
        ctx[j]["cums"] = cums

    def stage_gate_columns(j):
        project(j, OFF_G, IN_COLS_PAD)

    def stage_updates(j):
        cums = ctx[j].pop("cums")
        decays = []
        for c in range(n_chunks):
            t0 = c * CHUNK
            cum = cums[c]
            tot = cum[CHUNK - 1:CHUNK, :]
            decays.append(jnp.exp(tot))
            k_dec = (p_ref[j, t0:t0 + CHUNK, OFF_K:OFF_K + B_KEY]
                     * jnp.exp(tot - cum)).astype(BF16)
            v = p_ref[j, t0:t0 + CHUNK, OFF_VB:OFF_VB + B_WIDTH].astype(BF16)
            for pr in range(N_PAIRS):
                kl = slice(pr * PAIR_LANES, (pr + 1) * PAIR_LANES)
                vl = slice(pr * PAIR_ROWS, (pr + 1) * PAIR_ROWS)
                upd = lax.dot_general(v[:, vl], k_dec[:, kl], (((0,), (0,)), ((), ())),
                                      preferred_element_type=F32)
                upd_ref[j, c, pr] = jnp.where(same_head, upd, 0.0)
        ctx[j]["decays"] = decays

    def stage_gmlp(j):
        va = p_ref[j, :, OFF_VA:OFF_VA + A_WIDTH]
        vn = (va * _rms_scale(va) * nv_ref[...]).astype(BF16)
        for blk in range(rows // A_BLOCK):
            r0 = blk * A_BLOCK
            for hf in range(2):
                c0 = hf * half_width
                v_blk = vn[r0:r0 + A_BLOCK, c0:c0 + half_width]
                v_stack = jnp.concatenate(
                    [jnp.where(lane_group == g, v_blk, jnp.zeros_like(v_blk))
                     for g in range(half_groups)], axis=0)
                z = (jnp.dot(ws_half[hf], v_stack, preferred_element_type=F32)
                     + bs_ref[:, c0:c0 + half_width])
                u = p_ref[j, r0:r0 + A_BLOCK, OFF_U + c0:OFF_U + c0 + half_width]
                y_ref[j, r0:r0 + A_BLOCK, c0:c0 + half_width] = (u * z).astype(BF16)

    def stage_recurrence(j):
        decays = ctx[j].pop("decays")
        for pr in range(N_PAIRS):
            kl = slice(pr * PAIR_LANES, (pr + 1) * PAIR_LANES)
            st = st_ref[j, pr]
            for c in range(n_chunks):
                st = st * decays[c][:, kl] + upd_ref[j, c, pr]
                stb_ref[j, c, pr] = st.astype(BF16).T
            st_ref[j, pr] = st

    def stage_readout(j):
        for c in range(n_chunks):
            t0 = c * CHUNK
            for pr in range(N_PAIRS):
                q = (p_ref[j, t0:t0 + CHUNK,
                           OFF_Q + pr * PAIR_LANES:OFF_Q + (pr + 1) * PAIR_LANES]
                     * (B_HEAD_K ** -0.5)).astype(BF16)
                o = jnp.dot(q, stb_ref[j, c, pr], preferred_element_type=F32)
                p_ref[j, t0:t0 + CHUNK,
                      OFF_VB + pr * PAIR_ROWS:OFF_VB + (pr + 1) * PAIR_ROWS] = o

    def stage_out_a(j):
        ctx[j]["out_a"] = jnp.dot(y_ref[j, :, :A_WIDTH], wout_ref[:A_WIDTH, :].astype(BF16),
                                  preferred_element_type=F32)

    def stage_attention_gate(j):
        for hd in range(B_HEADS):
            cols = slice(hd * B_HEAD_V, (hd + 1) * B_HEAD_V)
            o = p_ref[j, :, OFF_VB + hd * B_HEAD_V:OFF_VB + (hd + 1) * B_HEAD_V]
            o = o * _rms_scale(o) * no_ref[:, cols]
            g = p_ref[j, :, OFF_G + hd * B_HEAD_V:OFF_G + (hd + 1) * B_HEAD_V]
            y_ref[j, :, A_WIDTH + hd * B_HEAD_V:A_WIDTH + (hd + 1) * B_HEAD_V] = (
                o * (g * _sigmoid(g))).astype(BF16)

    def stage_out_b(j):
        out_b = jnp.dot(y_ref[j, :, A_WIDTH:], wout_ref[A_WIDTH:, :].astype(BF16),
                        preferred_element_type=F32)
        o_ref[j] = x_ref[j] + (ctx[j].pop("out_a") + out_b)

    head = (stage_norm, stage_rank_columns, stage_attention_columns, stage_gate,
            stage_gmlp_columns, stage_cumulative_decay)
    for stage in head:
        for j in range(n_seqs):
            stage(j)
    for j in range(n_seqs):
        stage_updates(j)
        stage_gate_columns(j)
    for j in range(n_seqs):
        stage_gmlp(j)
        stage_recurrence(j)
    for j in range(n_seqs):
        stage_out_a(j)
        stage_readout(j)
    for stage in (stage_attention_gate, stage_out_b):
        for j in range(n_seqs):
            stage(j)


def mixer_call(x, norm_w, w_in_p, norm_v, w_s, b_s_full, w_gk_p, b_gk, norm_o, w_out, *, layer):
    bsz, seq, _ = x.shape
    n_chunks = MIX_ROWS // CHUNK
    return pl.pallas_call(
        _mixer_kernel,
        out_shape=jax.ShapeDtypeStruct(x.shape, F32),
        grid=(bsz // MIX_SEQS, seq // MIX_ROWS),
        in_specs=[
            pl.BlockSpec((MIX_SEQS, MIX_ROWS, D_MODEL), lambda b, s: (b, s, 0)),
            _resident((1, D_MODEL)),
            _resident(w_in_p.shape),
            _resident((1, A_WIDTH)),
            _resident(w_s.shape),
            _resident(b_s_full.shape),
            _resident(w_gk_p.shape),
            _resident((1, B_KEY)),
            _resident((1, B_WIDTH)),
            _layer_resident(w_out.shape, layer),
        ],
        out_specs=pl.BlockSpec((MIX_SEQS, MIX_ROWS, D_MODEL), lambda b, s: (b, s, 0)),
        scratch_shapes=[
            pltpu.VMEM((MIX_SEQS, MIX_ROWS, D_MODEL), BF16),
            pltpu.VMEM((MIX_SEQS, MIX_ROWS, IN_COLS_PAD), F32),
            pltpu.VMEM((MIX_SEQS, MIX_ROWS, D_MODEL), BF16),
            pltpu.VMEM((MIX_SEQS, n_chunks, N_PAIRS, PAIR_ROWS, PAIR_LANES), F32),
            pltpu.VMEM((MIX_SEQS, n_chunks, N_PAIRS, PAIR_LANES, PAIR_ROWS), BF16),
            pltpu.VMEM((MIX_SEQS, N_PAIRS, PAIR_ROWS, PAIR_LANES), F32),
        ],
        compiler_params=pltpu.CompilerParams(
            dimension_semantics=("arbitrary", "arbitrary"), vmem_limit_bytes=VMEM_LIMIT),
        name="mixer",
    )(x, norm_w, w_in_p, norm_v, w_s, b_s_full, w_gk_p, b_gk, norm_o, w_out)


def _regroup_w_in(w):
    a2, bk, bw = 2 * A_WIDTH, B_KEY, B_WIDTH
    qkv = w[:, a2:a2 + 2 * bk + bw]
    g = w[:, a2 + 2 * bk + bw:a2 + 2 * bk + 2 * bw]
    r = jnp.pad(w[:, a2 + 2 * bk + 2 * bw:], ((0, 0), (0, R_PAD - GATE_RANK)))
    return jnp.concatenate([r, qkv, w[:, :a2], g], axis=1).astype(BF16)


def kernel(x, ff1_norm, ff1_w_in, ff1_w_out, mix_norm, w_in, gmlp_norm_v, gmlp_w_s, gmlp_b_s,
           gla_w_gk2, gla_b_gk, gla_norm_o, w_out, ff2_norm, ff2_w_in, ff2_w_out, final_norm):
    bsz, seq, d = x.shape
    depth = ff1_norm.shape[0]
    row = lambda v: v.reshape(1, -1)
    rows3 = lambda v: v.reshape(depth, 1, -1)
    fw = row(final_norm)
    ff1_nw, ff2_nw = rows3(ff1_norm), rows3(ff2_norm)
    for l in range(depth):
        w_in_p = _regroup_w_in(w_in[l])
        w_gk_p = jnp.pad(gla_w_gk2[l], ((0, R_PAD - GATE_RANK), (0, 0))).astype(BF16)
        b_s_full = jnp.repeat(gmlp_b_s[l].T, A_GROUP_DIM, axis=1)

        x = ffn_call(x.reshape(bsz * seq, d), ff1_nw, ff1_w_in, ff1_w_out, fw,
                     layer=l, final_norm=False).reshape(bsz, seq, d)
        x = mixer_call(x, row(mix_norm[l]), w_in_p, row(gmlp_norm_v[l]), gmlp_w_s[l], b_s_full,
                       w_gk_p, row(gla_b_gk[l]), row(gla_norm_o[l]), w_out, layer=l)
        x = ffn_call(x.reshape(bsz * seq, d), ff2_nw, ff2_w_in, ff2_w_out, fw,
                     layer=l, final_norm=(l == depth - 1)).reshape(bsz, seq, d)
    return x
```
